```python
import jax, jax.numpy as jnp
from jax import lax
import numpy as np

D_MODEL = 2048
BATCH = 16
SEQ = 2048
DEPTH = 4

CHUNK = 64
N_MEM = 256
D_MIX = D_MODEL
POOL_W = D_MIX // 2
POOL_GROUPS = 4
POOL_GROUP_W = POOL_W // POOL_GROUPS
POOL_WINDOWS = (2, 4, 8, 16)
MLSTM_W = D_MIX - POOL_W
MLSTM_HEADS = 4
MLSTM_HEAD_DIM = MLSTM_W // MLSTM_HEADS
CONV_W = 4
D_IN = POOL_W + 4 * MLSTM_W + 2 * MLSTM_HEADS
XATTN_HEADS = 4
XATTN_HEAD_DIM = D_MODEL // XATTN_HEADS
D_FF = ((8 * D_MODEL // 3 + 255) // 256) * 256
EPS = 1e-6

kernel_name = "hymba_pool_mlstm_macaron_xattn"


def rms_norm(x, g):
    xf = x.astype(jnp.float32)
    y = xf * lax.rsqrt(jnp.mean(xf * xf, axis=-1, keepdims=True) + EPS)
    return (y * g.astype(jnp.float32)).astype(x.dtype)


def swiglu(x, w_gate, w_up, w_down):
    return (jax.nn.silu(x @ w_gate) * (x @ w_up)) @ w_down


def causal_dwconv(x, w):
    K = w.shape[0]
    S = x.shape[1]
    xp = jnp.pad(x, ((0, 0), (K - 1, 0), (0, 0)))
    y = xp[:, 0:S] * w[0]
    for j in range(1, K):
        y = y + xp[:, j:j + S] * w[j]
    return y


def multiscale_pool(p, pool_w, pool_scale):
    B, S, _ = p.shape
    pg = p.reshape(B, S, POOL_GROUPS, POOL_GROUP_W).astype(jnp.float32)
    cs = jnp.cumsum(pg, axis=1)
    t = jnp.arange(1, S + 1, dtype=jnp.float32)
    outs = []
    for g, w in enumerate(POOL_WINDOWS):
        c = cs[:, :, g]
        lag = jnp.pad(c[:, :-w], ((0, 0), (w, 0), (0, 0)))
        mean = (c - lag) / jnp.minimum(t, float(w))[None, :, None]
        outs.append(mean - pg[:, :, g])
    d = jnp.stack(outs, axis=2).astype(p.dtype)
    y = jnp.einsum('bsgc,gcd->bsgd', d, pool_w)
    return y.reshape(B, S, POOL_W) * pool_scale


def mlstm_chunkwise(q, k, v, log_i, log_f):
    B, S, H, dk = q.shape
    dv = v.shape[-1]
    nc = S // CHUNK

    def to_chunks(a):
        a = a.reshape((B, nc, CHUNK) + a.shape[2:])
        return jnp.moveaxis(jnp.moveaxis(a, 1, 0), 3, 2)

    qc, kc, vc, lic, lfc = (to_chunks(a) for a in (q, k, v, log_i, log_f))
    tril = jnp.tril(jnp.ones((CHUNK, CHUNK), dtype=bool))

    def step(carry, xs):
        C, n, m = carry
        qb, kb, vb, li, lf = xs
        b = jnp.cumsum(lf, axis=-1)
        Dm = b[..., :, None] - b[..., None, :] + li[..., None, :]
        Dm = jnp.where(tril, Dm, -jnp.inf)
        inter = b + m[..., None]
        m_t = jnp.maximum(inter, jnp.max(Dm, axis=-1))
        W = jnp.exp(Dm - m_t[..., None]) * jnp.einsum('bhtd,bhsd->bhts', qb, kb)
        iw = jnp.exp(inter - m_t)
        num = iw[..., None] * jnp.einsum('bhtk,bhkv->bhtv', qb, C) + jnp.einsum('bhts,bhsv->bhtv', W, vb)
        den = iw * jnp.einsum('bhtk,bhk->bht', qb, n) + jnp.sum(W, axis=-1)
        h = num / jnp.maximum(jnp.abs(den), jnp.exp(-m_t))[..., None]
        bL = b[..., -1]
        g = bL[..., None] - b + li
        m_new = jnp.maximum(bL + m, jnp.max(g, axis=-1))
        decay = jnp.exp(bL + m - m_new)
        ws = jnp.exp(g - m_new[..., None])
        C_new = decay[..., None, None] * C + jnp.einsum('bhs,bhsk,bhsv->bhkv', ws, kb, vb)
        n_new = decay[..., None] * n + jnp.einsum('bhs,bhsk->bhk', ws, kb)
        return (C_new, n_new, m_new), h

    init = (jnp.zeros((B, H, dk, dv), jnp.float32),
            jnp.zeros((B, H, dk), jnp.float32),
            jnp.zeros((B, H), jnp.float32))
    _, hs = lax.scan(step, init, (qc, kc, vc, lic, lfc))
    return jnp.transpose(hs, (1, 0, 3, 2, 4)).reshape(B, S, H, dv)


def token_mix(u, w_in, gate_bias, qk_conv, head_norm, pool_w, pool_scale, w_out):
    B, S, _ = u.shape
    H, dh = MLSTM_HEADS, MLSTM_HEAD_DIM
    z = u @ w_in
    o1 = POOL_W
    o2 = o1 + 2 * MLSTM_W
    o3 = o2 + MLSTM_W
    o4 = o3 + MLSTM_W
    o5 = o4 + H
    p, qk, v, og, gi, gf = z[..., :o1], z[..., o1:o2], z[..., o2:o3], z[..., o3:o4], z[..., o4:o5], z[..., o5:]
    y_pool = multiscale_pool(p, pool_w, pool_scale)
    qk = jax.nn.silu(causal_dwconv(qk, qk_conv)).astype(jnp.float32)
    q = qk[..., :MLSTM_W].reshape(B, S, H, dh) * (dh ** -0.5)
    k = qk[..., MLSTM_W:].reshape(B, S, H, dh)
    vh = v.astype(jnp.float32).reshape(B, S, H, dh)
    gb = gate_bias.astype(jnp.float32)
    log_i = gi.astype(jnp.float32) + gb[:H]
    log_f = jax.nn.log_sigmoid(gf.astype(jnp.float32) + gb[H:])
    h = mlstm_chunkwise(q, k, vh, log_i, log_f)
    h = h * lax.rsqrt(jnp.mean(h * h, axis=-1, keepdims=True) + EPS)
    h = h * head_norm.astype(jnp.float32).reshape(H, dh)
    y_mlstm = h.reshape(B, S, MLSTM_W).astype(u.dtype) * jax.nn.sigmoid(og)
    return jnp.concatenate([y_pool, y_mlstm], axis=-1) @ w_out


def cross_attention(u, mem_n, wq, wkv, wo):
    B, S, _ = u.shape
    q = (u @ wq).reshape(B, S, XATTN_HEADS, XATTN_HEAD_DIM)
    kv = (mem_n @ wkv).reshape(B, mem_n.shape[1], 2, XATTN_HEADS, XATTN_HEAD_DIM)
    k, v = kv[:, :, 0], kv[:, :, 1]
    s = jnp.einsum('bshd,bmhd->bhsm', q, k).astype(jnp.float32) * (XATTN_HEAD_DIM ** -0.5)
    pr = jax.nn.softmax(s, axis=-1).astype(v.dtype)
    o = jnp.einsum('bhsm,bmhd->bshd', pr, v).reshape(B, S, D_MODEL)
    return o @ wo


def setup_inputs(seed: int = 0) -> dict:
    key = jax.random.key(seed)
    ks = jax.random.split(key, 32)
    L, D, F = DEPTH, D_MODEL, D_FF

    def dense(k, shape, fan_in):
        return jax.random.normal(k, shape, jnp.float32) * (fan_in ** -0.5)

    def gain(k, shape):
        return 1.0 + 0.05 * jax.random.normal(k, shape, jnp.float32)

    f_bias = jnp.linspace(3.0, 6.0, MLSTM_HEADS, dtype=jnp.float32)
    gate_bias = jnp.concatenate([
        0.1 * jax.random.normal(ks[20], (L, MLSTM_HEADS), jnp.float32),
        f_bias[None, :] + 0.1 * jax.random.normal(ks[21], (L, MLSTM_HEADS), jnp.float32)], axis=-1)
    return {
        "x": jax.random.normal(ks[0], (BATCH, SEQ, D), jnp.float32),
        "mem": jax.random.normal(ks[1], (BATCH, N_MEM, D), jnp.float32),
        "ffn1_norm": gain(ks[2], (L, D)),
        "ffn1_w_gate": dense(ks[3], (L, D, F), D),
        "ffn1_w_up": dense(ks[4], (L, D, F), D),
        "ffn1_w_down": dense(ks[5], (L, F, D), F),
        "mix_norm": gain(ks[6], (L, D)),
        "w_in": dense(ks[7], (L, D, D_IN), D),
        "gate_bias": gate_bias,
        "qk_conv": dense(ks[8], (L, CONV_W, 2 * MLSTM_W), CONV_W),
        "head_norm": gain(ks[9], (L, MLSTM_W)),
        "pool_w": dense(ks[10], (L, POOL_GROUPS, POOL_GROUP_W, POOL_GROUP_W), POOL_GROUP_W),
        "pool_scale": gain(ks[11], (L, POOL_W)),
        "w_out": dense(ks[12], (L, D_MIX, D), D_MIX),
        "xattn_norm": gain(ks[13], (L, D)),
        "mem_norm": gain(ks[14], (L, D)),
        "xattn_wq": dense(ks[15], (L, D, D), D),
        "xattn_wkv": dense(ks[16], (L, D, 2 * D), D),
        "xattn_wo": dense(ks[17], (L, D, D), D),
        "ffn2_norm": gain(ks[18], (L, D)),
        "ffn2_w_gate": dense(ks[19], (L, D, F), D),
        "ffn2_w_up": dense(ks[22], (L, D, F), D),
        "ffn2_w_down": dense(ks[23], (L, F, D), F),
        "final_norm": gain(ks[24], (D,)),
    }


def reference(x, mem, ffn1_norm, ffn1_w_gate, ffn1_w_up, ffn1_w_down, mix_norm, w_in, gate_bias,
              qk_conv, head_norm, pool_w, pool_scale, w_out, xattn_norm, mem_norm, xattn_wq,
              xattn_wkv, xattn_wo, ffn2_norm, ffn2_w_gate, ffn2_w_up, ffn2_w_down, final_norm):
    h = x
    for l in range(DEPTH):
        h = h + 0.5 * swiglu(rms_norm(h, ffn1_norm[l]), ffn1_w_gate[l], ffn1_w_up[l], ffn1_w_down[l])
        h = h + token_mix(rms_norm(h, mix_norm[l]), w_in[l], gate_bias[l], qk_conv[l], head_norm[l],
                          pool_w[l], pool_scale[l], w_out[l])
        h = h + cross_attention(rms_norm(h, xattn_norm[l]), rms_norm(mem, mem_norm[l]),
                                xattn_wq[l], xattn_wkv[l], xattn_wo[l])
        h = h + 0.5 * swiglu(rms_norm(h, ffn2_norm[l]), ffn2_w_gate[l], ffn2_w_up[l], ffn2_w_down[l])
    return rms_norm(h, final_norm)
```

```python
import functools

import jax
import jax.numpy as jnp
from jax import lax
from jax.experimental import pallas as pl
from jax.experimental.pallas import tpu as pltpu

F32 = jnp.float32
BF16 = jnp.bfloat16

EPS = 1e-6
MLSTM_CHUNK = 64
MLSTM_HEADS = 4
POOL_WINDOWS = (2, 4, 8, 16)
CONV_W = 4
XATTN_HEADS = 4
LANES = 128
POOL_HALO = 16
CONV_HALO = 8
VMEM_LIMIT = 56 * 1024 * 1024


def _tile(n, pref):
    return pref if n % pref == 0 else n


def _rms(x, g):
    y = x * lax.rsqrt(jnp.mean(x * x, axis=-1, keepdims=True) + EPS)
    return y * g


def _silu(x):
    return x * jax.nn.sigmoid(x)


def _log_sigmoid(x):
    return jnp.minimum(x, 0.0) - jnp.log1p(jnp.exp(-jnp.abs(x)))


def _params(*sem):
    return pltpu.CompilerParams(dimension_semantics=sem, vmem_limit_bytes=VMEM_LIMIT)


def _ffn_kernel(h_ref, g_ref, wg_ref, wu_ref, wd_ref, fg_ref, o_ref, xn_ref, *, final):
    j = pl.program_id(1)

    @pl.when(j == 0)
    def _():
        x = h_ref[...]
        xn_ref[...] = _rms(x, g_ref[...]).astype(BF16)
        o_ref[...] = x

    xn = xn_ref[...]
    a = jnp.dot(xn, wg_ref[...], preferred_element_type=F32)
    b = jnp.dot(xn, wu_ref[...], preferred_element_type=F32)
    mid = (_silu(a) * b).astype(BF16)
    o_ref[...] += 0.5 * jnp.dot(mid, wd_ref[...], preferred_element_type=F32)

    if final:
        @pl.when(j == pl.num_programs(1) - 1)
        def _():
            o_ref[...] = _rms(o_ref[...], fg_ref[...])


def _ffn(h, g, wg, wu, wd, fg, layer, *, final):
    T, D = h.shape
    F = wg.shape[-1]
    tm = _tile(T, 512)
    tf = _tile(F, 512)
    return pl.pallas_call(
        functools.partial(_ffn_kernel, final=final),
        grid=(T // tm, F // tf),
        in_specs=[
            pl.BlockSpec((tm, D), lambda i, j: (i, 0)),
            pl.BlockSpec((None, 1, D), lambda i, j: (layer, 0, 0)),
            pl.BlockSpec((None, D, tf), lambda i, j: (layer, 0, j)),
            pl.BlockSpec((None, D, tf), lambda i, j: (layer, 0, j)),
            pl.BlockSpec((None, tf, D), lambda i, j: (layer, j, 0)),
            pl.BlockSpec((1, D), lambda i, j: (0, 0)),
        ],
        out_specs=pl.BlockSpec((tm, D), lambda i, j: (i, 0)),
        out_shape=jax.ShapeDtypeStruct((T, D), F32),
        scratch_shapes=[pltpu.VMEM((tm, D), BF16)],
        compiler_params=_params("parallel", "arbitrary"),
        name="ffn_final" if final else "ffn",
    )(h, g, wg, wu, wd, fg)


def _inproj_kernel(h_ref, g_ref, w_ref, wgate_ref, zf_ref, zb_ref, gate_ref, xn_ref, *, nf):
    j = pl.program_id(1)

    @pl.when(j == 0)
    def _():
        xn = _rms(h_ref[...], g_ref[...]).astype(BF16)
        xn_ref[...] = xn
        gate_ref[...] = jnp.dot(xn, wgate_ref[...], preferred_element_type=F32)

    z = jnp.dot(xn_ref[...], w_ref[...], preferred_element_type=F32)

    @pl.when(j < nf)
    def _():
        zf_ref[...] = z

    @pl.when(j >= nf)
    def _():
        zb_ref[...] = z.astype(BF16)


def _inproj(h, g, w_main, w_gate, layer):
    T, D = h.shape
    W = D // 2
    nf, nb = 3, 2
    tm = _tile(T, 512)
    return pl.pallas_call(
        functools.partial(_inproj_kernel, nf=nf),
        grid=(T // tm, nf + nb),
        in_specs=[
            pl.BlockSpec((tm, D), lambda i, j: (i, 0)),
            pl.BlockSpec((None, 1, D), lambda i, j: (layer, 0, 0)),
            pl.BlockSpec((None, D, W), lambda i, j: (layer, 0, j)),
            pl.BlockSpec((None, D, 2 * LANES), lambda i, j: (layer, 0, 0)),
        ],
        out_specs=[
            pl.BlockSpec((tm, W), lambda i, j: (i, jnp.minimum(j, nf - 1))),
            pl.BlockSpec((tm, W), lambda i, j: (i, jnp.maximum(j - nf, 0))),
            pl.BlockSpec((tm, 2 * LANES), lambda i, j: (i, 0)),
        ],
        out_shape=[
            jax.ShapeDtypeStruct((T, nf * W), F32),
            jax.ShapeDtypeStruct((T, nb * W), BF16),
            jax.ShapeDtypeStruct((T, 2 * LANES), F32),
        ],
        scratch_shapes=[pltpu.VMEM((tm, D), BF16)],
        compiler_params=_params("parallel", "arbitrary"),
        name="mix_inproj",
    )(h, g, w_main, w_gate)


def _seg_scan(x, op, fill, r, seg):
    sh = 1
    while sh < seg:
        shifted = pltpu.roll(x, sh, axis=0)
        x = op(x, jnp.where(r >= sh, shifted, fill))
        sh *= 2
    return x


def _mix_kernel(p_ref, q_ref, k_ref, v_ref, og_ref, gate_ref, h_ref,
                gbias_ref, conv_ref, hn_ref, poolw_ref, pscale_ref, wout_ref,
                o_ref,
                phalo_ref, chalo_ref, c_ref, n_ref, m_ref,
                qs_ref, ks_ref, y_ref, b_ref, a_ref, ca_ref, at_ref,
                *, ts, W):
    L = MLSTM_CHUNK
    H = MLSTM_HEADS
    dh = W // H
    gw = W // len(POOL_WINDOWS)
    nc = ts // L
    s = pl.program_id(1)

    @pl.when(s == 0)
    def _():
        phalo_ref[...] = jnp.zeros_like(phalo_ref)
        chalo_ref[...] = jnp.zeros_like(chalo_ref)
        c_ref[...] = jnp.zeros_like(c_ref)
        n_ref[...] = jnp.zeros_like(n_ref)
        m_ref[...] = jnp.zeros_like(m_ref)

    row = lax.broadcasted_iota(jnp.int32, (ts, 1), 0)

    tpos = row + s * ts + 1
    for g, win in enumerate(POOL_WINDOWS):
        cs = slice(g * gw, (g + 1) * gw)
        pg = p_ref[:, cs]
        acc = jnp.concatenate([phalo_ref[:, cs], pg], axis=0)
        sh = 1
        while sh < win:
            acc = acc + pltpu.roll(acc, sh, axis=0)
            sh *= 2
        cnt = jnp.minimum(tpos, win).astype(F32)
        d = acc[POOL_HALO:] / cnt - pg
        yg = jnp.dot(d.astype(BF16), poolw_ref[g], preferred_element_type=F32)
        y_ref[:, cs] = (yg * pscale_ref[:, cs]).astype(BF16)
    phalo_ref[...] = p_ref[ts - POOL_HALO:, :]

    nblk = W // dh
    for blk in range(2 * nblk):
        src = q_ref if blk < nblk else k_ref
        cs = slice((blk % nblk) * dh, (blk % nblk + 1) * dh)
        hs = slice(blk * dh, (blk + 1) * dh)
        x = src[:, cs]
        ext = jnp.concatenate([chalo_ref[:, hs], x], axis=0)
        y = pltpu.roll(ext, CONV_W - 1, axis=0) * conv_ref[0:1, hs]
        for j in range(1, CONV_W):
            sh = CONV_W - 1 - j
            xj = pltpu.roll(ext, sh, axis=0) if sh else ext
            y = y + xj * conv_ref[j:j + 1, hs]
        y = _silu(y[CONV_HALO:])
        chalo_ref[:, hs] = x[ts - CONV_HALO:, :]
        if blk < nblk:
            qs_ref[:, cs] = (y * (dh ** -0.5)).astype(BF16)
        else:
            ks_ref[:, cs] = y

    gates = gate_ref[...] + gbias_ref[...]
    li = gates[:, :LANES]
    lf = _log_sigmoid(gates[:, LANES:])
    r = row % L
    b = _seg_scan(lf, jnp.add, 0.0, r, L)
    a = li - b
    ca = _seg_scan(a, jnp.maximum, -jnp.inf, r, L)
    b_ref[...] = b
    a_ref[...] = a
    ca_ref[...] = ca
    a_t = a.T
    for c in range(nc):
        at_ref[c] = a_t[0:8, c * L:(c + 1) * L]

    tri = (lax.broadcasted_iota(jnp.int32, (L, L), 0)
           >= lax.broadcasted_iota(jnp.int32, (L, L), 1))

    def chunk_body(c, carry):
        rows = pl.ds(pl.multiple_of(c * L, L), L)
        b_c = b_ref[rows, :]
        a_c = a_ref[rows, :]
        m_prev = m_ref[...]
        u = jnp.maximum(m_prev, ca_ref[rows, :])
        u_l = u[L - 1:L, :]
        m_ref[...] = b_c[L - 1:L, :] + u_l
        iw = jnp.exp(m_prev - u)
        emt = jnp.exp(-b_c - u)
        ws = jnp.exp(a_c - u_l)
        decay = jnp.exp(m_prev - u_l)
        a_row = at_ref[c]
        for hd in range(H):
            cs = slice(hd * dh, (hd + 1) * dh)
            q_c = qs_ref[rows, cs]
            k_c = ks_ref[rows, cs]
            v_c = v_ref[rows, cs]
            sc = lax.dot_general(q_c, k_c.astype(BF16), (((1,), (1,)), ((), ())),
                                 preferred_element_type=F32)
            e = jnp.where(tri, jnp.exp(a_row[hd:hd + 1, :] - u[:, hd:hd + 1]), 0.0)
            wm = e * sc
            cst = c_ref[hd]
            nst = n_ref[hd]
            iw_h = iw[:, hd:hd + 1]
            num = (iw_h * jnp.dot(q_c, cst.astype(BF16), preferred_element_type=F32)
                   + jnp.dot(wm.astype(BF16), v_c, preferred_element_type=F32))
            den = (iw_h * jnp.sum(q_c.astype(F32) * nst, axis=-1, keepdims=True)
                   + jnp.sum(wm, axis=-1, keepdims=True))
            hh = num / jnp.maximum(jnp.abs(den), emt[:, hd:hd + 1])
            hh = hh * lax.rsqrt(jnp.mean(hh * hh, axis=-1, keepdims=True) + EPS)
            hh = hh * hn_ref[:, cs]
            yv = hh * jax.nn.sigmoid(og_ref[rows, cs].astype(F32))
            y_ref[rows, W + hd * dh:W + (hd + 1) * dh] = yv.astype(BF16)
            kw = k_c * ws[:, hd:hd + 1]
            dec_h = decay[:, hd:hd + 1]
            c_ref[hd] = dec_h * cst + lax.dot_general(
                kw.astype(BF16), v_c, (((0,), (0,)), ((), ())), preferred_element_type=F32)
            n_ref[hd] = dec_h * nst + jnp.sum(kw, axis=0, keepdims=True)
        return carry

    lax.fori_loop(0, nc, chunk_body, 0)

    o_ref[...] = h_ref[...] + jnp.dot(y_ref[...], wout_ref[...], preferred_element_type=F32)


def _mix(h, zf, zb, gates, gbias, conv_w, head_norm, pool_w, pool_scale, w_out, layer, *, B, S):
    T, D = h.shape
    W = D // 2
    H = MLSTM_HEADS
    dh = W // H
    ng = len(POOL_WINDOWS)
    gw = W // ng
    ts = _tile(S, 256)
    nst = S // ts
    nc = ts // MLSTM_CHUNK
    rowmap = lambda b, s: b * nst + s
    return pl.pallas_call(
        functools.partial(_mix_kernel, ts=ts, W=W),
        grid=(B, nst),
        in_specs=[
            pl.BlockSpec((ts, W), lambda b, s: (rowmap(b, s), 0)),
            pl.BlockSpec((ts, W), lambda b, s: (rowmap(b, s), 1)),
            pl.BlockSpec((ts, W), lambda b, s: (rowmap(b, s), 2)),
            pl.BlockSpec((ts, W), lambda b, s: (rowmap(b, s), 0)),
            pl.BlockSpec((ts, W), lambda b, s: (rowmap(b, s), 1)),
            pl.BlockSpec((ts, 2 * LANES), lambda b, s: (rowmap(b, s), 0)),
            pl.BlockSpec((ts, D), lambda b, s: (rowmap(b, s), 0)),
            pl.BlockSpec((None, 1, 2 * LANES), lambda b, s: (layer, 0, 0)),
            pl.BlockSpec((None, CONV_W, 2 * W), lambda b, s: (layer, 0, 0)),
            pl.BlockSpec((None, 1, W), lambda b, s: (layer, 0, 0)),
            pl.BlockSpec((None, ng, gw, gw), lambda b, s: (layer, 0, 0, 0)),
            pl.BlockSpec((None, 1, W), lambda b, s: (layer, 0, 0)),
            pl.BlockSpec((None, D, D), lambda b, s: (layer, 0, 0)),
        ],
        out_specs=pl.BlockSpec((ts, D), lambda b, s: (rowmap(b, s), 0)),
        out_shape=jax.ShapeDtypeStruct((T, D), F32),
        scratch_shapes=[
            pltpu.VMEM((POOL_HALO, W), F32),
            pltpu.VMEM((CONV_HALO, 2 * W), F32),
            pltpu.VMEM((H, dh, dh), F32),
            pltpu.VMEM((H, 1, dh), F32),
            pltpu.VMEM((1, LANES), F32),
            pltpu.VMEM((ts, W), BF16),
            pltpu.VMEM((ts, W), F32),
            pltpu.VMEM((ts, D), BF16),
            pltpu.VMEM((ts, LANES), F32),
            pltpu.VMEM((ts, LANES), F32),
            pltpu.VMEM((ts, LANES), F32),
            pltpu.VMEM((nc, 8, MLSTM_CHUNK), F32),
        ],
        compiler_params=_params("arbitrary", "arbitrary"),
        name="mix",
    )(zf, zf, zf, zb, zb, gates, h, gbias, conv_w, head_norm, pool_w, pool_scale, w_out)


def _kv_kernel(m_ref, g_ref, w_ref, o_ref, xn_ref):
    j = pl.program_id(1)

    @pl.when(j == 0)
    def _():
        xn_ref[...] = _rms(m_ref[...], g_ref[...]).astype(BF16)

    o_ref[...] = jnp.dot(xn_ref[...], w_ref[...], preferred_element_type=F32).astype(BF16)


def _kv_proj(mem, g, wkv, layer):
    T, D = mem.shape
    N = wkv.shape[-1]
    tm = _tile(T, 512)
    tn = _tile(N, 1024)
    return pl.pallas_call(
        _kv_kernel,
        grid=(T // tm, N // tn),
        in_specs=[
            pl.BlockSpec((tm, D), lambda i, j: (i, 0)),
            pl.BlockSpec((None, 1, D), lambda i, j: (layer, 0, 0)),
            pl.BlockSpec((None, D, tn), lambda i, j: (layer, 0, j)),
        ],
        out_specs=pl.BlockSpec((tm, tn), lambda i, j: (i, j)),
        out_shape=jax.ShapeDtypeStruct((T, N), BF16),
        scratch_shapes=[pltpu.VMEM((tm, D), BF16)],
        compiler_params=_params("parallel", "arbitrary"),
        name="xattn_kv",
    )(mem, g, wkv)


def _xattn_kernel(h_ref, g_ref, wq_ref, k_ref, v_ref, wo_ref, o_ref, xn_ref, *, scale):
    hd = pl.program_id(1)

    @pl.when(hd == 0)
    def _():
        x = h_ref[...]
        xn_ref[...] = _rms(x, g_ref[...]).astype(BF16)
        o_ref[...] = x

    q = jnp.dot(xn_ref[...], wq_ref[...], preferred_element_type=F32)
    s = lax.dot_general(q.astype(BF16), k_ref[...], (((1,), (1,)), ((), ())),
                        preferred_element_type=F32) * scale
    e = jnp.exp(s - jnp.max(s, axis=-1, keepdims=True))
    pr = e / jnp.sum(e, axis=-1, keepdims=True)
    o = jnp.dot(pr.astype(BF16), v_ref[...], preferred_element_type=F32)
    o_ref[...] += jnp.dot(o.astype(BF16), wo_ref[...], preferred_element_type=F32)


def _xattn(h, g, wq, kv, wo, layer, *, S, M):
    T, D = h.shape
    H = XATTN_HEADS
    dh = D // H
    tm = _tile(S, 512)
    nsb = S // tm
    return pl.pallas_call(
        functools.partial(_xattn_kernel, scale=dh ** -0.5),
        grid=(T // tm, H),
        in_specs=[
            pl.BlockSpec((tm, D), lambda i, j: (i, 0)),
            pl.BlockSpec((None, 1, D), lambda i, j: (layer, 0, 0)),
            pl.BlockSpec((None, D, dh), lambda i, j: (layer, 0, j)),
            pl.BlockSpec((M, dh), lambda i, j: (i // nsb, j)),
            pl.BlockSpec((M, dh), lambda i, j: (i // nsb, H + j)),
            pl.BlockSpec((None, dh, D), lambda i, j: (layer, j, 0)),
        ],
        out_specs=pl.BlockSpec((tm, D), lambda i, j: (i, 0)),
        out_shape=jax.ShapeDtypeStruct((T, D), F32),
        scratch_shapes=[pltpu.VMEM((tm, D), BF16)],
        compiler_params=_params("parallel", "arbitrary"),
        name="xattn",
    )(h, g, wq, kv, kv, wo)


def kernel(x, mem, ffn1_norm, ffn1_w_gate, ffn1_w_up, ffn1_w_down, mix_norm, w_in, gate_bias, qk_conv, head_norm, pool_w, pool_scale, w_out, xattn_norm, mem_norm, xattn_wq, xattn_wkv, xattn_wo, ffn2_norm, ffn2_w_gate, ffn2_w_up, ffn2_w_down, final_norm):
    B, S, D = x.shape
    M = mem.shape[1]
    depth = ffn1_norm.shape[0]
    W = D // 2
    H = MLSTM_HEADS
    n_main = 5 * W

    bf = lambda w: w.astype(BF16)
    row = lambda v: v.reshape(v.shape[0], 1, v.shape[-1])

    f1g, f1u, f1d = bf(ffn1_w_gate), bf(ffn1_w_up), bf(ffn1_w_down)
    f2g, f2u, f2d = bf(ffn2_w_gate), bf(ffn2_w_up), bf(ffn2_w_down)
    w_main = bf(w_in[:, :, :n_main])
    w_gate = jnp.zeros((depth, D, 2 * LANES), BF16)
    w_gate = w_gate.at[:, :, :H].set(bf(w_in[:, :, n_main:n_main + H]))
    w_gate = w_gate.at[:, :, LANES:LANES + H].set(bf(w_in[:, :, n_main + H:]))
    gbias = jnp.zeros((depth, 1, 2 * LANES), F32)
    gbias = gbias.at[:, 0, :H].set(gate_bias[:, :H])
    gbias = gbias.at[:, 0, LANES:LANES + H].set(gate_bias[:, H:])
    wout, poolw = bf(w_out), bf(pool_w)
    wq, wkv, wo = bf(xattn_wq), bf(xattn_wkv), bf(xattn_wo)
    fg = final_norm.reshape(1, D)

    h = x.reshape(B * S, D)
    mem2 = mem.reshape(B * M, D)
    for l in range(depth):
        h = _ffn(h, row(ffn1_norm), f1g, f1u, f1d, fg, l, final=False)
        zf, zb, gates = _inproj(h, row(mix_norm), w_main, w_gate, l)
        h = _mix(h, zf, zb, gates, gbias, qk_conv, row(head_norm), poolw, row(pool_scale),
                 wout, l, B=B, S=S)
        kv = _kv_proj(mem2, row(mem_norm), wkv, l)
        h = _xattn(h, row(xattn_norm), wq, kv, wo, l, S=S, M=M)
        h = _ffn(h, row(ffn2_norm), f2g, f2u, f2d, fg, l, final=(l == depth - 1))
    return h.reshape(B, S, D)
```

```python
import functools

import jax
import jax.numpy as jnp
from jax import lax
from jax.experimental import pallas as pl
from jax.experimental.pallas import tpu as pltpu

F32 = jnp.float32
BF16 = jnp.bfloat16

EPS = 1e-6
MLSTM_HEADS = 4
POOL_WINDOWS = (2, 4, 8, 16)
CONV_W = 4
XATTN_HEADS = 4
LANES = 128
POOL_HALO = 32
CONV_HALO = 8
MIB = 1024 * 1024

FFN_TM, FFN_TF, FFN_VMEM = 1024, 512, 58 * MIB
MIX_TILE, MIX_VMEM = 256, 56 * MIB
XATTN_TM, XATTN_VMEM = 512, 56 * MIB
KV_TM, KV_TN, KV_VMEM = 512, 1024, 32 * MIB


def _tile(n, pref):
    return pref if n % pref == 0 else n


def _rms(x, g):
    y = x * lax.rsqrt(jnp.mean(x * x, axis=-1, keepdims=True) + EPS)
    return y * g


def _silu(x):
    return x * jax.nn.sigmoid(x)


def _log_sigmoid(x):
    return jnp.minimum(x, 0.0) - jnp.log1p(jnp.exp(-jnp.abs(x)))


def _params(vmem, *sem):
    return pltpu.CompilerParams(dimension_semantics=sem, vmem_limit_bytes=vmem)


_RESIDENT = dict(pipeline_mode=pl.Buffered(1))


def _ffn_kernel(h_ref, g_ref, wg_ref, wu_ref, wd_ref, fg_ref, o_ref, xn_ref, *, final):
    j = pl.program_id(1)

    @pl.when(j == 0)
    def _():
        x = h_ref[...]
        xn_ref[...] = _rms(x, g_ref[...]).astype(BF16)
        o_ref[...] = x

    xn = xn_ref[...]
    a = jnp.dot(xn, wg_ref[...], preferred_element_type=F32)
    b = jnp.dot(xn, wu_ref[...], preferred_element_type=F32)
    mid = (_silu(a) * b).astype(BF16)
    o_ref[...] += 0.5 * jnp.dot(mid, wd_ref[...], preferred_element_type=F32)

    if final:
        @pl.when(j == pl.num_programs(1) - 1)
        def _():
            o_ref[...] = _rms(o_ref[...], fg_ref[...])


def _ffn(h, g, wg, wu, wd, fg, layer, *, final):
    T, D = h.shape
    F = wg.shape[-1]
    tm = _tile(T, FFN_TM)
    tf = _tile(F, FFN_TF)
    return pl.pallas_call(
        functools.partial(_ffn_kernel, final=final),
        grid=(T // tm, F // tf),
        in_specs=[
            pl.BlockSpec((tm, D), lambda i, j: (i, 0)),
            pl.BlockSpec((None, 1, D), lambda i, j: (layer, 0, 0)),
            pl.BlockSpec((None, D, tf), lambda i, j: (layer, 0, j)),
            pl.BlockSpec((None, D, tf), lambda i, j: (layer, 0, j)),
            pl.BlockSpec((None, tf, D), lambda i, j: (layer, j, 0)),
            pl.BlockSpec((1, D), lambda i, j: (0, 0)),
        ],
        out_specs=pl.BlockSpec((tm, D), lambda i, j: (i, 0)),
        out_shape=jax.ShapeDtypeStruct((T, D), F32),
        scratch_shapes=[pltpu.VMEM((tm, D), BF16)],
        compiler_params=_params(FFN_VMEM, "parallel", "arbitrary"),
        name="ffn_final" if final else "ffn",
    )(h, g, wg, wu, wd, fg)


def _scan_rows(x, op, fill, row):
    n = x.shape[0]
    sh = 1
    while sh < n:
        x = op(x, jnp.where(row >= sh, pltpu.roll(x, sh, axis=0), fill))
        sh *= 2
    return x


def _mix_kernel(h_ref, g_ref, win_ref, wgate_ref, gbias_ref, conv_ref, hn_ref, poolw_ref,
                pscale_ref, wout_ref,
                o_ref,
                pext_ref, s2_ref, s4_ref, cext_ref, c_ref, n_ref, m_ref,
                xn_ref, qs_ref, ks_ref, y_ref,
                *, ts, W):
    H = MLSTM_HEADS
    dh = W // H
    gw = W // len(POOL_WINDOWS)
    PH, CH = POOL_HALO, CONV_HALO
    s = pl.program_id(1)

    @pl.when(s == 0)
    def _():
        pext_ref[0:PH, :] = jnp.zeros((PH, W), F32)
        cext_ref[0:CH, :] = jnp.zeros((CH, 2 * W), F32)
        c_ref[...] = jnp.zeros_like(c_ref)
        n_ref[...] = jnp.zeros_like(n_ref)
        m_ref[...] = jnp.zeros_like(m_ref)

    x = h_ref[...]
    xn_ref[...] = _rms(x, g_ref[...]).astype(BF16)
    row = lax.broadcasted_iota(jnp.int32, (ts, 1), 0)

    def proj(lo, n):
        return jnp.dot(xn_ref[...], win_ref[:, lo:lo + n], preferred_element_type=F32)

    pext_ref[PH:PH + ts, :] = proj(0, W)
    tpos = row + s * ts + 1
    n_ext = PH + ts
    for g, win in enumerate(POOL_WINDOWS):
        cs = slice(g * gw, (g + 1) * gw)
        pg = pext_ref[PH:n_ext, cs]
        if win == 2:
            acc = pg + pext_ref[PH - 1:n_ext - 1, cs]
        else:
            s2_ref[8:n_ext, :] = pext_ref[8:n_ext, cs] + pext_ref[7:n_ext - 1, cs]
            if win == 4:
                acc = s2_ref[PH:n_ext, :] + s2_ref[PH - 2:n_ext - 2, :]
            else:
                s4_ref[16:n_ext, :] = s2_ref[16:n_ext, :] + s2_ref[14:n_ext - 2, :]
                if win == 8:
                    acc = s4_ref[PH:n_ext, :] + s4_ref[PH - 4:n_ext - 4, :]
                else:
                    s8 = s4_ref[24:n_ext, :] + s4_ref[20:n_ext - 4, :]
                    acc = s8[8:] + s8[:-8]
        inv_cnt = 1.0 / jnp.minimum(tpos, win).astype(F32)
        d = acc * inv_cnt - pg
        yg = jnp.dot(d.astype(BF16), poolw_ref[g], preferred_element_type=F32)
        y_ref[:, cs] = (yg * pscale_ref[:, cs]).astype(BF16)
    pext_ref[0:PH, :] = pext_ref[ts:ts + PH, :]

    nblk = W // dh
    for blk in range(2 * nblk):
        hs = slice(blk * dh, (blk + 1) * dh)
        cext_ref[CH:CH + ts, hs] = proj(W + blk * dh, dh)
        lo = CH - (CONV_W - 1)
        y = cext_ref[lo:lo + ts, hs] * conv_ref[0:1, hs]
        for j in range(1, CONV_W):
            y = y + cext_ref[lo + j:lo + j + ts, hs] * conv_ref[j:j + 1, hs]
        y = _silu(y)
        cext_ref[0:CH, hs] = cext_ref[ts:ts + CH, hs]
        if blk < nblk:
            qs_ref[:, hs] = (y * (dh ** -0.5)).astype(BF16)
        else:
            ks_ref[:, (blk - nblk) * dh:(blk - nblk + 1) * dh] = y

    gates = jnp.dot(xn_ref[...], wgate_ref[...], preferred_element_type=F32) + gbias_ref[...]
    li = gates[:, :LANES]
    lf = _log_sigmoid(gates[:, LANES:])
    b = _scan_rows(lf, jnp.add, 0.0, row)
    a = li - b
    m_prev = m_ref[...]
    u = jnp.maximum(m_prev, _scan_rows(a, jnp.maximum, -jnp.inf, row))
    u_l = u[ts - 1:ts, :]
    m_ref[...] = b[ts - 1:ts, :] + u_l
    iw = jnp.exp(m_prev - u)
    emt = jnp.exp(-b - u)
    ws = jnp.exp(a - u_l)
    decay = jnp.exp(m_prev - u_l)
    a_row = a.T
    tri = (lax.broadcasted_iota(jnp.int32, (ts, ts), 0)
           >= lax.broadcasted_iota(jnp.int32, (ts, ts), 1))

    for hd in range(H):
        cs = slice(hd * dh, (hd + 1) * dh)
        q_c = qs_ref[:, cs]
        k_c = ks_ref[:, cs]
        v_c = proj(3 * W + hd * dh, dh).astype(BF16)
        og = proj(4 * W + hd * dh, dh)
        sc = lax.dot_general(q_c, k_c.astype(BF16), (((1,), (1,)), ((), ())),
                             preferred_element_type=F32)
        e = jnp.where(tri, jnp.exp(a_row[hd:hd + 1, :] - u[:, hd:hd + 1]), 0.0)
        wm = e * sc
        cst = c_ref[hd]
        nst = n_ref[hd]
        iw_h = iw[:, hd:hd + 1]
        num = (iw_h * jnp.dot(q_c, cst.astype(BF16), preferred_element_type=F32)
               + jnp.dot(wm.astype(BF16), v_c, preferred_element_type=F32))
        den = (iw_h * jnp.sum(q_c.astype(F32) * nst, axis=-1, keepdims=True)
               + jnp.sum(wm, axis=-1, keepdims=True))
        hh = num * (1.0 / jnp.maximum(jnp.abs(den), emt[:, hd:hd + 1]))
        hh = hh * lax.rsqrt(jnp.mean(hh * hh, axis=-1, keepdims=True) + EPS)
        yv = hh * hn_ref[:, cs] * jax.nn.sigmoid(og)
        y_ref[:, W + hd * dh:W + (hd + 1) * dh] = yv.astype(BF16)
        kw = k_c * ws[:, hd:hd + 1]
        dec_h = decay[:, hd:hd + 1]
        c_ref[hd] = dec_h * cst + lax.dot_general(
            kw.astype(BF16), v_c, (((0,), (0,)), ((), ())), preferred_element_type=F32)
        n_ref[hd] = dec_h * nst + jnp.sum(kw, axis=0, keepdims=True)

    o_ref[...] = x + jnp.dot(y_ref[...], wout_ref[...], preferred_element_type=F32)


def _mix(h, g, w_main, w_gate, gbias, conv_w, head_norm, pool_w, pool_scale, w_out, layer, *, B, S):
    T, D = h.shape
    W = D // 2
    H = MLSTM_HEADS
    dh = W // H
    ng = len(POOL_WINDOWS)
    gw = W // ng
    ts = _tile(S, MIX_TILE)
    nst = S // ts
    return pl.pallas_call(
        functools.partial(_mix_kernel, ts=ts, W=W),
        grid=(B, nst),
        in_specs=[
            pl.BlockSpec((ts, D), lambda b, s: (b * nst + s, 0)),
            pl.BlockSpec((None, 1, D), lambda b, s: (layer, 0, 0)),
            pl.BlockSpec((None, D, 5 * W), lambda b, s: (layer, 0, 0), **_RESIDENT),
            pl.BlockSpec((None, D, 2 * LANES), lambda b, s: (layer, 0, 0), **_RESIDENT),
            pl.BlockSpec((None, 1, 2 * LANES), lambda b, s: (layer, 0, 0)),
            pl.BlockSpec((None, CONV_W, 2 * W), lambda b, s: (layer, 0, 0)),
            pl.BlockSpec((None, 1, W), lambda b, s: (layer, 0, 0)),
            pl.BlockSpec((None, ng, gw, gw), lambda b, s: (layer, 0, 0, 0)),
            pl.BlockSpec((None, 1, W), lambda b, s: (layer, 0, 0)),
            pl.BlockSpec((None, D, D), lambda b, s: (layer, 0, 0), **_RESIDENT),
        ],
        out_specs=pl.BlockSpec((ts, D), lambda b, s: (b * nst + s, 0)),
        out_shape=jax.ShapeDtypeStruct((T, D), F32),
        scratch_shapes=[
            pltpu.VMEM((POOL_HALO + ts, W), F32),
            pltpu.VMEM((POOL_HALO + ts, gw), F32),
            pltpu.VMEM((POOL_HALO + ts, gw), F32),
            pltpu.VMEM((CONV_HALO + ts, 2 * W), F32),
            pltpu.VMEM((H, dh, dh), F32),
            pltpu.VMEM((H, 1, dh), F32),
            pltpu.VMEM((1, LANES), F32),
            pltpu.VMEM((ts, D), BF16),
            pltpu.VMEM((ts, W), BF16),
            pltpu.VMEM((ts, W), F32),
            pltpu.VMEM((ts, D), BF16),
        ],
        compiler_params=_params(MIX_VMEM, "arbitrary", "arbitrary"),
        name="mix",
    )(h, g, w_main, w_gate, gbias, conv_w, head_norm, pool_w, pool_scale, w_out)


def _kv_kernel(m_ref, g_ref, w_ref, o_ref, xn_ref):
    j = pl.program_id(1)

    @pl.when(j == 0)
    def _():
        xn_ref[...] = _rms(m_ref[...], g_ref[...]).astype(BF16)

    o_ref[...] = jnp.dot(xn_ref[...], w_ref[...], preferred_element_type=F32).astype(BF16)


def _kv_proj(mem, g, wkv, layer):
    T, D = mem.shape
    N = wkv.shape[-1]
    tm = _tile(T, KV_TM)
    tn = _tile(N, KV_TN)
    return pl.pallas_call(
        _kv_kernel,
        grid=(T // tm, N // tn),
        in_specs=[
            pl.BlockSpec((tm, D), lambda i, j: (i, 0)),
            pl.BlockSpec((None, 1, D), lambda i, j: (layer, 0, 0)),
            pl.BlockSpec((None, D, tn), lambda i, j: (layer, 0, j)),
        ],
        out_specs=pl.BlockSpec((tm, tn), lambda i, j: (i, j)),
        out_shape=jax.ShapeDtypeStruct((T, N), BF16),
        scratch_shapes=[pltpu.VMEM((tm, D), BF16)],
        compiler_params=_params(KV_VMEM, "parallel", "arbitrary"),
        name="xattn_kv",
    )(mem, g, wkv)


def _xattn_kernel(h_ref, g_ref, wq_ref, kv_ref, wo_ref, o_ref, xn_ref, ao_ref, *, D):
    H = XATTN_HEADS
    dh = D // H
    x = h_ref[...]
    xn_ref[...] = _rms(x, g_ref[...]).astype(BF16)
    for hd in range(H):
        cs = slice(hd * dh, (hd + 1) * dh)
        q = jnp.dot(xn_ref[...], wq_ref[:, cs], preferred_element_type=F32)
        s = lax.dot_general(q.astype(BF16), kv_ref[:, cs], (((1,), (1,)), ((), ())),
                            preferred_element_type=F32) * (dh ** -0.5)
        e = jnp.exp(s - jnp.max(s, axis=-1, keepdims=True))
        pr = e * (1.0 / jnp.sum(e, axis=-1, keepdims=True))
        ao_ref[:, cs] = jnp.dot(pr.astype(BF16), kv_ref[:, D + hd * dh:D + (hd + 1) * dh],
                                preferred_element_type=F32).astype(BF16)
    o_ref[...] = x + jnp.dot(ao_ref[...], wo_ref[...], preferred_element_type=F32)


def _xattn(h, g, wq, kv, wo, layer, *, S, M):
    T, D = h.shape
    tm = _tile(S, XATTN_TM)
    nsb = S // tm
    return pl.pallas_call(
        functools.partial(_xattn_kernel, D=D),
        grid=(T // tm,),
        in_specs=[
            pl.BlockSpec((tm, D), lambda i: (i, 0)),
            pl.BlockSpec((None, 1, D), lambda i: (layer, 0, 0)),
            pl.BlockSpec((None, D, D), lambda i: (layer, 0, 0), **_RESIDENT),
            pl.BlockSpec((M, 2 * D), lambda i: (i // nsb, 0)),
            pl.BlockSpec((None, D, D), lambda i: (layer, 0, 0), **_RESIDENT),
        ],
        out_specs=pl.BlockSpec((tm, D), lambda i: (i, 0)),
        out_shape=jax.ShapeDtypeStruct((T, D), F32),
        scratch_shapes=[pltpu.VMEM((tm, D), BF16), pltpu.VMEM((tm, D), BF16)],
        compiler_params=_params(XATTN_VMEM, "parallel"),
        name="xattn",
    )(h, g, wq, kv, wo)


def kernel(x, mem, ffn1_norm, ffn1_w_gate, ffn1_w_up, ffn1_w_down, mix_norm, w_in, gate_bias, qk_conv, head_norm, pool_w, pool_scale, w_out, xattn_norm, mem_norm, xattn_wq, xattn_wkv, xattn_wo, ffn2_norm, ffn2_w_gate, ffn2_w_up, ffn2_w_down, final_norm):
    B, S, D = x.shape
    M = mem.shape[1]
    depth = ffn1_norm.shape[0]
    W = D // 2
    H = MLSTM_HEADS
    n_main = 5 * W

    bf = lambda w: w.astype(BF16)
    row = lambda v: v.reshape(v.shape[0], 1, v.shape[-1])

    f1g, f1u, f1d = bf(ffn1_w_gate), bf(ffn1_w_up), bf(ffn1_w_down)
    f2g, f2u, f2d = bf(ffn2_w_gate), bf(ffn2_w_up), bf(ffn2_w_down)
    w_main = bf(w_in[:, :, :n_main])
    w_gate = jnp.zeros((depth, D, 2 * LANES), BF16)
    w_gate = w_gate.at[:, :, :H].set(bf(w_in[:, :, n_main:n_main + H]))
    w_gate = w_gate.at[:, :, LANES:LANES + H].set(bf(w_in[:, :, n_main + H:]))
    gbias = jnp.zeros((depth, 1, 2 * LANES), F32)
    gbias = gbias.at[:, 0, :H].set(gate_bias[:, :H])
    gbias = gbias.at[:, 0, LANES:LANES + H].set(gate_bias[:, H:])
    wout, poolw = bf(w_out), bf(pool_w)
    wq, wkv, wo = bf(xattn_wq), bf(xattn_wkv), bf(xattn_wo)
    fg = final_norm.reshape(1, D)

    h = x.reshape(B * S, D)
    mem2 = mem.reshape(B * M, D)
    for l in range(depth):
        h = _ffn(h, row(ffn1_norm), f1g, f1u, f1d, fg, l, final=False)
        h = _mix(h, row(mix_norm), w_main, w_gate, gbias, qk_conv, row(head_norm), poolw,
                 row(pool_scale), wout, l, B=B, S=S)
        kv = _kv_proj(mem2, row(mem_norm), wkv, l)
        h = _xattn(h, row(xattn_norm), wq, kv, wo, l, S=S, M=M)
        h = _ffn(h, row(ffn2_norm), f2g, f2u, f2d, fg, l, final=(l == depth - 1))
    return h.reshape(B, S, D)
```

```python
import functools

import jax
import jax.numpy as jnp
from jax import lax
from jax.experimental import pallas as pl
from jax.experimental.pallas import tpu as pltpu

F32 = jnp.float32
BF16 = jnp.bfloat16

EPS = 1e-6
MLSTM_HEADS = 4
POOL_WINDOWS = (2, 4, 8, 16)
CONV_W = 4
XATTN_HEADS = 4
LANES = 128
POOL_HALO = 32
CONV_HALO = 8
MIB = 1024 * 1024

FFN_TM, FFN_TF, FFN_VMEM = 1024, 512, 60 * MIB
MIX_TILE, MIX_VMEM = 256, 56 * MIB
XATTN_TM, XATTN_VMEM = 512, 58 * MIB
KV_TM, KV_TN, KV_VMEM = 512, 1024, 32 * MIB


def _tile(n, pref):
    return pref if n % pref == 0 else n


def _rms(x, g):
    y = x * lax.rsqrt(jnp.mean(x * x, axis=-1, keepdims=True) + EPS)
    return y * g


def _silu(x):
    return x * jax.nn.sigmoid(x)


def _log_sigmoid(x):
    return jnp.minimum(x, 0.0) - jnp.log1p(jnp.exp(-jnp.abs(x)))


def _params(vmem, *sem):
    return pltpu.CompilerParams(dimension_semantics=sem, vmem_limit_bytes=vmem)


_RESIDENT = dict(pipeline_mode=pl.Buffered(1))


def _ffn_kernel(*refs, final, nc):
    h_ref, g_ref, wg_ref, wu_ref, wd_ref, fg_ref = refs[:6]
    src_refs = refs[6:6 + nc]
    o_ref = refs[6 + nc]
    dst_refs = refs[7 + nc:7 + 2 * nc]
    xn_ref = refs[7 + 2 * nc]
    j = pl.program_id(1)

    @pl.when(j == 0)
    def _():
        x = h_ref[...]
        xn_ref[...] = _rms(x, g_ref[...]).astype(BF16)
        o_ref[...] = x

    xn = xn_ref[...]
    a = jnp.dot(xn, wg_ref[...], preferred_element_type=F32)
    b = jnp.dot(xn, wu_ref[...], preferred_element_type=F32)
    mid = (_silu(a) * b).astype(BF16)
    o_ref[...] += 0.5 * jnp.dot(mid, wd_ref[...], preferred_element_type=F32)

    _copy_casts(src_refs, dst_refs)

    if final:
        @pl.when(j == pl.num_programs(1) - 1)
        def _():
            o_ref[...] = _rms(o_ref[...], fg_ref[...])


def _cast_job(w, layer, grid, ncol=1):
    _, R, C = w.shape
    ni = grid[0]
    if len(grid) == 1 or ncol > grid[1] or C % ncol or (C // ncol) % LANES:
        ncol = 1
    assert R % ni == 0
    if len(grid) == 1:
        idx = lambda i: (i, 0)
    else:
        idx = lambda i, j: (i, jnp.minimum(j, ncol - 1))
    src = pl.BlockSpec((None, R // ni, C // ncol), lambda *ij: (layer,) + idx(*ij))
    dst = pl.BlockSpec((R // ni, C // ncol), idx)
    return src, dst, jax.ShapeDtypeStruct((R, C), BF16)


def _copy_casts(src_refs, dst_refs):
    for src_ref, dst_ref in zip(src_refs, dst_refs):
        dst_ref[...] = src_ref[...].astype(BF16)


def _ffn(h, g, wg, wu, wd, fg, layer, casts, *, final):
    T, D = h.shape
    F = wg.shape[-1]
    tm = _tile(T, FFN_TM)
    tf = _tile(F, FFN_TF)
    ni, nj = T // tm, F // tf
    jobs = [_cast_job(w, wl, (ni, nj), nj) for w, wl in casts]
    outs = pl.pallas_call(
        functools.partial(_ffn_kernel, final=final, nc=len(jobs)),
        grid=(ni, nj),
        in_specs=[
            pl.BlockSpec((tm, D), lambda i, j: (i, 0)),
            pl.BlockSpec((None, 1, D), lambda i, j: (layer, 0, 0)),
            pl.BlockSpec((D, tf), lambda i, j: (0, j)),
            pl.BlockSpec((D, tf), lambda i, j: (0, j)),
            pl.BlockSpec((tf, D), lambda i, j: (j, 0)),
            pl.BlockSpec((1, D), lambda i, j: (0, 0)),
        ] + [jb[0] for jb in jobs],
        out_specs=[pl.BlockSpec((tm, D), lambda i, j: (i, 0))] + [jb[1] for jb in jobs],
        out_shape=[jax.ShapeDtypeStruct((T, D), F32)] + [jb[2] for jb in jobs],
        scratch_shapes=[pltpu.VMEM((tm, D), BF16)],
        compiler_params=_params(FFN_VMEM, "parallel", "arbitrary"),
        name="ffn_final" if final else "ffn",
    )(h, g, wg, wu, wd, fg, *[c[0] for c in casts])
    return outs[0], list(outs[1:])


def _scan_rows(x, op, fill, row):
    n = x.shape[0]
    sh = 1
    while sh < n:
        x = op(x, jnp.where(row >= sh, pltpu.roll(x, sh, axis=0), fill))
        sh *= 2
    return x


def _mix_kernel(h_ref, g_ref, win_ref, wgate_ref, gbias_ref, conv_ref, hn_ref, poolw_ref,
                pscale_ref, wout_ref,
                o_ref,
                pext_ref, s2_ref, s4_ref, cext_ref, c_ref, n_ref, m_ref,
                xn_ref, qs_ref, ks_ref, y_ref,
                *, ts, W):
    H = MLSTM_HEADS
    dh = W // H
    gw = W // len(POOL_WINDOWS)
    PH, CH = POOL_HALO, CONV_HALO
    s = pl.program_id(1)

    @pl.when(s == 0)
    def _():
        pext_ref[0:PH, :] = jnp.zeros((PH, W), F32)
        cext_ref[0:CH, :] = jnp.zeros((CH, 2 * W), F32)
        c_ref[...] = jnp.zeros_like(c_ref)
        n_ref[...] = jnp.zeros_like(n_ref)
        m_ref[...] = jnp.zeros_like(m_ref)

    x = h_ref[...]
    xn_ref[...] = _rms(x, g_ref[...]).astype(BF16)
    row = lax.broadcasted_iota(jnp.int32, (ts, 1), 0)

    def proj(lo, n):
        return jnp.dot(xn_ref[...], win_ref[:, lo:lo + n], preferred_element_type=F32)

    pext_ref[PH:PH + ts, :] = proj(0, W)
    tpos = row + s * ts + 1
    n_ext = PH + ts
    for g, win in enumerate(POOL_WINDOWS):
        cs = slice(g * gw, (g + 1) * gw)
        pg = pext_ref[PH:n_ext, cs]
        if win == 2:
            acc = pg + pext_ref[PH - 1:n_ext - 1, cs]
        else:
            s2_ref[8:n_ext, :] = pext_ref[8:n_ext, cs] + pext_ref[7:n_ext - 1, cs]
            if win == 4:
                acc = s2_ref[PH:n_ext, :] + s2_ref[PH - 2:n_ext - 2, :]
            else:
                s4_ref[16:n_ext, :] = s2_ref[16:n_ext, :] + s2_ref[14:n_ext - 2, :]
                if win == 8:
                    acc = s4_ref[PH:n_ext, :] + s4_ref[PH - 4:n_ext - 4, :]
                else:
                    s8 = s4_ref[24:n_ext, :] + s4_ref[20:n_ext - 4, :]
                    acc = s8[8:] + s8[:-8]
        inv_cnt = 1.0 / jnp.minimum(tpos, win).astype(F32)
        d = acc * inv_cnt - pg
        yg = jnp.dot(d.astype(BF16), poolw_ref[g], preferred_element_type=F32)
        y_ref[:, cs] = (yg * pscale_ref[:, cs]).astype(BF16)
    pext_ref[0:PH, :] = pext_ref[ts:ts + PH, :]

    nblk = W // dh
    for blk in range(2 * nblk):
        hs = slice(blk * dh, (blk + 1) * dh)
        cext_ref[CH:CH + ts, hs] = proj(W + blk * dh, dh)
        lo = CH - (CONV_W - 1)
        y = cext_ref[lo:lo + ts, hs] * conv_ref[0:1, hs]
        for j in range(1, CONV_W):
            y = y + cext_ref[lo + j:lo + j + ts, hs] * conv_ref[j:j + 1, hs]
        y = _silu(y)
        cext_ref[0:CH, hs] = cext_ref[ts:ts + CH, hs]
        if blk < nblk:
            qs_ref[:, hs] = (y * (dh ** -0.5)).astype(BF16)
        else:
            ks_ref[:, (blk - nblk) * dh:(blk - nblk + 1) * dh] = y

    gates = jnp.dot(xn_ref[...], wgate_ref[...], preferred_element_type=F32) + gbias_ref[...]
    li = gates[:, :LANES]
    lf = _log_sigmoid(gates[:, LANES:])
    b = _scan_rows(lf, jnp.add, 0.0, row)
    a = li - b
    m_prev = m_ref[...]
    u = jnp.maximum(m_prev, _scan_rows(a, jnp.maximum, -jnp.inf, row))
    u_l = u[ts - 1:ts, :]
    m_ref[...] = b[ts - 1:ts, :] + u_l
    iw = jnp.exp(m_prev - u)
    emt = jnp.exp(-b - u)
    ws = jnp.exp(a - u_l)
    decay = jnp.exp(m_prev - u_l)
    a_row = a.T
    tri = (lax.broadcasted_iota(jnp.int32, (ts, ts), 0)
           >= lax.broadcasted_iota(jnp.int32, (ts, ts), 1))

    for hd in range(H):
        cs = slice(hd * dh, (hd + 1) * dh)
        q_c = qs_ref[:, cs]
        k_c = ks_ref[:, cs]
        v_c = proj(3 * W + hd * dh, dh).astype(BF16)
        og = proj(4 * W + hd * dh, dh)
        sc = lax.dot_general(q_c, k_c.astype(BF16), (((1,), (1,)), ((), ())),
                             preferred_element_type=F32)
        e = jnp.where(tri, jnp.exp(a_row[hd:hd + 1, :] - u[:, hd:hd + 1]), 0.0)
        wm = e * sc
        cst = c_ref[hd]
        nst = n_ref[hd]
        iw_h = iw[:, hd:hd + 1]
        num = (iw_h * jnp.dot(q_c, cst.astype(BF16), preferred_element_type=F32)
               + jnp.dot(wm.astype(BF16), v_c, preferred_element_type=F32))
        den = (iw_h * jnp.sum(q_c.astype(F32) * nst, axis=-1, keepdims=True)
               + jnp.sum(wm, axis=-1, keepdims=True))
        hh = num * (1.0 / jnp.maximum(jnp.abs(den), emt[:, hd:hd + 1]))
        hh = hh * lax.rsqrt(jnp.mean(hh * hh, axis=-1, keepdims=True) + EPS)
        yv = hh * hn_ref[:, cs] * jax.nn.sigmoid(og)
        y_ref[:, W + hd * dh:W + (hd + 1) * dh] = yv.astype(BF16)
        kw = k_c * ws[:, hd:hd + 1]
        dec_h = decay[:, hd:hd + 1]
        c_ref[hd] = dec_h * cst + lax.dot_general(
            kw.astype(BF16), v_c, (((0,), (0,)), ((), ())), preferred_element_type=F32)
        n_ref[hd] = dec_h * nst + jnp.sum(kw, axis=0, keepdims=True)

    o_ref[...] = x + jnp.dot(y_ref[...], wout_ref[...], preferred_element_type=F32)


def _mix(h, g, w_main, w_gate, gbias, conv_w, head_norm, pool_w, pool_scale, w_out, layer, *, B, S):
    T, D = h.shape
    W = D // 2
    H = MLSTM_HEADS
    dh = W // H
    ng = len(POOL_WINDOWS)
    gw = W // ng
    ts = _tile(S, MIX_TILE)
    nst = S // ts
    return pl.pallas_call(
        functools.partial(_mix_kernel, ts=ts, W=W),
        grid=(B, nst),
        in_specs=[
            pl.BlockSpec((ts, D), lambda b, s: (b * nst + s, 0)),
            pl.BlockSpec((None, 1, D), lambda b, s: (layer, 0, 0)),
            pl.BlockSpec((D, 5 * W), lambda b, s: (0, 0), **_RESIDENT),
            pl.BlockSpec((None, D, 2 * LANES), lambda b, s: (layer, 0, 0), **_RESIDENT),
            pl.BlockSpec((None, 1, 2 * LANES), lambda b, s: (layer, 0, 0)),
            pl.BlockSpec((None, CONV_W, 2 * W), lambda b, s: (layer, 0, 0)),
            pl.BlockSpec((None, 1, W), lambda b, s: (layer, 0, 0)),
            pl.BlockSpec((ng, gw, gw), lambda b, s: (0, 0, 0)),
            pl.BlockSpec((None, 1, W), lambda b, s: (layer, 0, 0)),
            pl.BlockSpec((D, D), lambda b, s: (0, 0), **_RESIDENT),
        ],
        out_specs=pl.BlockSpec((ts, D), lambda b, s: (b * nst + s, 0)),
        out_shape=jax.ShapeDtypeStruct((T, D), F32),
        scratch_shapes=[
            pltpu.VMEM((POOL_HALO + ts, W), F32),
            pltpu.VMEM((POOL_HALO + ts, gw), F32),
            pltpu.VMEM((POOL_HALO + ts, gw), F32),
            pltpu.VMEM((CONV_HALO + ts, 2 * W), F32),
            pltpu.VMEM((H, dh, dh), F32),
            pltpu.VMEM((H, 1, dh), F32),
            pltpu.VMEM((1, LANES), F32),
            pltpu.VMEM((ts, D), BF16),
            pltpu.VMEM((ts, W), BF16),
            pltpu.VMEM((ts, W), F32),
            pltpu.VMEM((ts, D), BF16),
        ],
        compiler_params=_params(MIX_VMEM, "arbitrary", "arbitrary"),
        name="mix",
    )(h, g, w_main, w_gate, gbias, conv_w, head_norm, pool_w, pool_scale, w_out)


def _kv_kernel(m_ref, g_ref, w_ref, o_ref, xn_ref):
    j = pl.program_id(1)

    @pl.when(j == 0)
    def _():
        xn_ref[...] = _rms(m_ref[...], g_ref[...]).astype(BF16)

    o_ref[...] = jnp.dot(xn_ref[...], w_ref[...], preferred_element_type=F32).astype(BF16)


def _kv_proj(mem, g, wkv, layer):
    T, D = mem.shape
    N = wkv.shape[-1]
    tm = _tile(T, KV_TM)
    tn = _tile(N, KV_TN)
    return pl.pallas_call(
        _kv_kernel,
        grid=(T // tm, N // tn),
        in_specs=[
            pl.BlockSpec((tm, D), lambda i, j: (i, 0)),
            pl.BlockSpec((None, 1, D), lambda i, j: (layer, 0, 0)),
            pl.BlockSpec((D, tn), lambda i, j: (0, j)),
        ],
        out_specs=pl.BlockSpec((tm, tn), lambda i, j: (i, j)),
        out_shape=jax.ShapeDtypeStruct((T, N), BF16),
        scratch_shapes=[pltpu.VMEM((tm, D), BF16)],
        compiler_params=_params(KV_VMEM, "parallel", "arbitrary"),
        name="xattn_kv",
    )(mem, g, wkv)


def _xattn_kernel(*refs, D, nc):
    h_ref, g_ref, wq_ref, kv_ref, wo_ref = refs[:5]
    o_ref = refs[5 + nc]
    xn_ref, ao_ref = refs[6 + 2 * nc:]
    H = XATTN_HEADS
    dh = D // H
    x = h_ref[...]
    xn_ref[...] = _rms(x, g_ref[...]).astype(BF16)
    for hd in range(H):
        cs = slice(hd * dh, (hd + 1) * dh)
        q = jnp.dot(xn_ref[...], wq_ref[:, cs], preferred_element_type=F32)
        s = lax.dot_general(q.astype(BF16), kv_ref[:, cs], (((1,), (1,)), ((), ())),
                            preferred_element_type=F32) * (dh ** -0.5)
        e = jnp.exp(s - jnp.max(s, axis=-1, keepdims=True))
        pr = e * (1.0 / jnp.sum(e, axis=-1, keepdims=True))
        ao_ref[:, cs] = jnp.dot(pr.astype(BF16), kv_ref[:, D + hd * dh:D + (hd + 1) * dh],
                                preferred_element_type=F32).astype(BF16)
    o_ref[...] = x + jnp.dot(ao_ref[...], wo_ref[...], preferred_element_type=F32)
    _copy_casts(refs[5:5 + nc], refs[6 + nc:6 + 2 * nc])


def _xattn(h, g, wq, kv, wo, layer, casts, *, S, M):
    T, D = h.shape
    tm = _tile(S, XATTN_TM)
    nsb = S // tm
    grid = (T // tm,)
    jobs = [_cast_job(w, wl, grid) for w, wl in casts]
    outs = pl.pallas_call(
        functools.partial(_xattn_kernel, D=D, nc=len(jobs)),
        grid=grid,
        in_specs=[
            pl.BlockSpec((tm, D), lambda i: (i, 0)),
            pl.BlockSpec((None, 1, D), lambda i: (layer, 0, 0)),
            pl.BlockSpec((D, D), lambda i: (0, 0), **_RESIDENT),
            pl.BlockSpec((M, 2 * D), lambda i: (i // nsb, 0)),
            pl.BlockSpec((D, D), lambda i: (0, 0), **_RESIDENT),
        ] + [jb[0] for jb in jobs],
        out_specs=[pl.BlockSpec((tm, D), lambda i: (i, 0))] + [jb[1] for jb in jobs],
        out_shape=[jax.ShapeDtypeStruct((T, D), F32)] + [jb[2] for jb in jobs],
        scratch_shapes=[pltpu.VMEM((tm, D), BF16), pltpu.VMEM((tm, D), BF16)],
        compiler_params=_params(XATTN_VMEM, "parallel"),
        name="xattn",
    )(h, g, wq, kv, wo, *[c[0] for c in casts])
    return outs[0], list(outs[1:])


def kernel(x, mem, ffn1_norm, ffn1_w_gate, ffn1_w_up, ffn1_w_down, mix_norm, w_in, gate_bias, qk_conv, head_norm, pool_w, pool_scale, w_out, xattn_norm, mem_norm, xattn_wq, xattn_wkv, xattn_wo, ffn2_norm, ffn2_w_gate, ffn2_w_up, ffn2_w_down, final_norm):
    B, S, D = x.shape
    M = mem.shape[1]
    depth = ffn1_norm.shape[0]
    W = D // 2
    H = MLSTM_HEADS
    n_main = 5 * W

    bf = lambda w: w.astype(BF16)
    row = lambda v: v.reshape(v.shape[0], 1, v.shape[-1])
    rows_d = lambda w: w.reshape(depth, D, -1)

    lane_pad = lambda v: jnp.pad(v, [(0, 0)] * (v.ndim - 1) + [(0, LANES - H)])
    w_gate = bf(jnp.concatenate([lane_pad(w_in[:, :, n_main:n_main + H]),
                                 lane_pad(w_in[:, :, n_main + H:])], axis=-1))
    gbias = jnp.concatenate([lane_pad(gate_bias[:, :H]), lane_pad(gate_bias[:, H:])],
                            axis=-1).reshape(depth, 1, 2 * LANES)
    fg = final_norm.reshape(1, D)
    F = ffn1_w_gate.shape[-1]

    def ffn_w(wg, wu, wd, l):
        return [(wg, l), (wu, l), (rows_d(wd), l)]

    def mix_xattn_w(l):
        return [(w_in, l), (rows_d(pool_w), l), (w_out, l), (xattn_wq, l), (xattn_wkv, l), (xattn_wo, l)]

    fw = [bf(w[0]) for w, _ in ffn_w(ffn1_w_gate, ffn1_w_up, ffn1_w_down, 0)]
    mw = [bf(w[0]) for w, _ in mix_xattn_w(0)]
    h = x.reshape(B * S, D)
    mem2 = mem.reshape(B * M, D)
    for l in range(depth):
        last = l == depth - 1
        h, _ = _ffn(h, row(ffn1_norm), fw[0], fw[1], fw[2].reshape(F, D), fg, l, [], final=False)
        w_main, poolw, wout, wq, wkv, wo = mw
        h = _mix(h, row(mix_norm), w_main, w_gate, gbias, qk_conv, row(head_norm),
                 poolw.reshape(pool_w.shape[1:]), row(pool_scale), wout, l, B=B, S=S)
        kv = _kv_proj(mem2, row(mem_norm), wkv, l)
        casts = ffn_w(ffn2_w_gate, ffn2_w_up, ffn2_w_down, l) + ([] if last else mix_xattn_w(l + 1))
        h, cw = _xattn(h, row(xattn_norm), wq, kv, wo, l, casts, S=S, M=M)
        fw, mw = cw[:3], cw[3:]
        casts = [] if last else ffn_w(ffn1_w_gate, ffn1_w_up, ffn1_w_down, l + 1)
        h, cw = _ffn(h, row(ffn2_norm), fw[0], fw[1], fw[2].reshape(F, D), fg, l, casts, final=last)
        fw = cw
    return h.reshape(B, S, D)
```

```python
import functools

import jax
import jax.numpy as jnp
from jax import lax
from jax.experimental import pallas as pl
from jax.experimental.pallas import tpu as pltpu

F32 = jnp.float32
BF16 = jnp.bfloat16

EPS = 1e-6
MLSTM_HEADS = 4
POOL_WINDOWS = (2, 4, 8, 16)
CONV_W = 4
XATTN_HEADS = 4
LANES = 128
BF16_ROWS = 16
POOL_HALO = 32
CONV_HALO = 8
MIB = 1024 * 1024

FFN_TM, FFN_TF, FFN_VMEM = 1024, 512, 60 * MIB
MIX_TILE, MIX_VMEM = 256, 56 * MIB
XATTN_TM, XATTN_VMEM = 512, 58 * MIB
KV_TM, KV_TN, KV_VMEM = 512, 1024, 32 * MIB


def _tile(n, pref):
    return pref if n % pref == 0 else n


def _rms(x, g):
    y = x * lax.rsqrt(jnp.mean(x * x, axis=-1, keepdims=True) + EPS)
    return y * g


def _silu(x):
    return x * jax.nn.sigmoid(x)


def _log_sigmoid(x):
    return jnp.minimum(x, 0.0) - jnp.log1p(jnp.exp(-jnp.abs(x)))


def _params(vmem, *sem):
    return pltpu.CompilerParams(dimension_semantics=sem, vmem_limit_bytes=vmem)


_RESIDENT = dict(pipeline_mode=pl.Buffered(1))


def _ffn_kernel(*refs, final, nc):
    h_ref, g_ref, wg_ref, wu_ref, wd_ref, fg_ref = refs[:6]
    src_refs = refs[6:6 + nc]
    o_ref = refs[6 + nc]
    dst_refs = refs[7 + nc:7 + 2 * nc]
    xn_ref = refs[7 + 2 * nc]
    j = pl.program_id(1)

    @pl.when(j == 0)
    def _():
        x = h_ref[...]
        xn_ref[...] = _rms(x, g_ref[...]).astype(BF16)
        o_ref[...] = x

    xn = xn_ref[...]
    a = jnp.dot(xn, wg_ref[...], preferred_element_type=F32)
    b = jnp.dot(xn, wu_ref[...], preferred_element_type=F32)
    mid = (_silu(a) * b).astype(BF16)
    o_ref[...] += 0.5 * jnp.dot(mid, wd_ref[...], preferred_element_type=F32)

    _copy_casts(src_refs, dst_refs)

    if final:
        @pl.when(j == pl.num_programs(1) - 1)
        def _():
            o_ref[...] = _rms(o_ref[...], fg_ref[...])


def _cast_job(w, layer, grid):
    _, R, C = w.shape
    ni = grid[0]
    rep = 1
    while (R * rep) % (ni * BF16_ROWS) and rep < ni:
        rep *= 2
    nrb = ni // rep
    assert ni % rep == 0 and R % nrb == 0
    ncol = 1
    if len(grid) == 2:
        ncol = max([n for n in range(2, grid[1] + 1) if C % (n * LANES) == 0] + [1])
        idx = lambda i, j: (i // rep, jnp.minimum(j, ncol - 1))
    else:
        idx = lambda i: (i // rep, 0)
    src = pl.BlockSpec((None, R // nrb, C // ncol), lambda *ij: (layer,) + idx(*ij))
    dst = pl.BlockSpec((R // nrb, C // ncol), idx)
    return src, dst, jax.ShapeDtypeStruct((R, C), BF16)


def _cast_job_t(wt, layer, grid):
    _, C, R = wt.shape
    cpb = LANES * pl.cdiv(pl.cdiv(C, LANES), grid[0])
    last = pl.cdiv(C, cpb) - 1
    src = pl.BlockSpec((None, cpb, R), lambda i: (layer, jnp.minimum(i, last), 0))
    dst = pl.BlockSpec((R, cpb), lambda i: (0, jnp.minimum(i, last)))
    return src, dst, jax.ShapeDtypeStruct((R, C), BF16)


def _copy_casts(src_refs, dst_refs):
    for src_ref, dst_ref in zip(src_refs, dst_refs):
        v = src_ref[...]
        if src_ref.shape != dst_ref.shape:
            v = v.T
        dst_ref[...] = v.astype(BF16)


def _ffn(h, g, wg, wu, wd, fg, layer, casts, *, final):
    T, D = h.shape
    F = wg.shape[-1]
    tm = _tile(T, FFN_TM)
    tf = _tile(F, FFN_TF)
    ni, nj = T // tm, F // tf
    jobs = [_cast_job(w, wl, (ni, nj)) for w, wl, _ in casts]
    outs = pl.pallas_call(
        functools.partial(_ffn_kernel, final=final, nc=len(jobs)),
        grid=(ni, nj),
        in_specs=[
            pl.BlockSpec((tm, D), lambda i, j: (i, 0)),
            pl.BlockSpec((None, 1, D), lambda i, j: (layer, 0, 0)),
            pl.BlockSpec((D, tf), lambda i, j: (0, j)),
            pl.BlockSpec((D, tf), lambda i, j: (0, j)),
            pl.BlockSpec((tf, D), lambda i, j: (j, 0)),
            pl.BlockSpec((1, D), lambda i, j: (0, 0)),
        ] + [jb[0] for jb in jobs],
        out_specs=[pl.BlockSpec((tm, D), lambda i, j: (i, 0))] + [jb[1] for jb in jobs],
        out_shape=[jax.ShapeDtypeStruct((T, D), F32)] + [jb[2] for jb in jobs],
        scratch_shapes=[pltpu.VMEM((tm, D), BF16)],
        compiler_params=_params(FFN_VMEM, "parallel", "arbitrary"),
        name="ffn_final" if final else "ffn",
    )(h, g, wg, wu, wd, fg, *[c[0] for c in casts])
    return outs[0], list(outs[1:])


def _scan_rows(x, op, fill, row):
    n = x.shape[0]
    sh = 1
    while sh < n:
        x = op(x, jnp.where(row >= sh, pltpu.roll(x, sh, axis=0), fill))
        sh *= 2
    return x


def _mix_kernel(h_ref, g_ref, win_ref, wgate_ref, gbias_ref, conv_ref, hn_ref, poolw_ref,
                pscale_ref, wout_ref,
                o_ref,
                pext_ref, s2_ref, s4_ref, cext_ref, c_ref, n_ref, m_ref,
                xn_ref, qs_ref, ks_ref, y_ref,
                *, ts, W):
    H = MLSTM_HEADS
    dh = W // H
    gw = W // len(POOL_WINDOWS)
    PH, CH = POOL_HALO, CONV_HALO
    s = pl.program_id(1)

    @pl.when(s == 0)
    def _():
        pext_ref[0:PH, :] = jnp.zeros((PH, W), F32)
        cext_ref[0:CH, :] = jnp.zeros((CH, 2 * W), F32)
        c_ref[...] = jnp.zeros_like(c_ref)
        n_ref[...] = jnp.zeros_like(n_ref)
        m_ref[...] = jnp.zeros_like(m_ref)

    x = h_ref[...]
    xn_ref[...] = _rms(x, g_ref[...]).astype(BF16)
    row = lax.broadcasted_iota(jnp.int32, (ts, 1), 0)

    def proj(lo, n):
        return jnp.dot(xn_ref[...], win_ref[:, lo:lo + n], preferred_element_type=F32)

    pext_ref[PH:PH + ts, :] = proj(0, W)
    tpos = row + s * ts + 1
    n_ext = PH + ts
    for g, win in enumerate(POOL_WINDOWS):
        cs = slice(g * gw, (g + 1) * gw)
        pg = pext_ref[PH:n_ext, cs]
        if win == 2:
            acc = pg + pext_ref[PH - 1:n_ext - 1, cs]
        else:
            s2_ref[8:n_ext, :] = pext_ref[8:n_ext, cs] + pext_ref[7:n_ext - 1, cs]
            if win == 4:
                acc = s2_ref[PH:n_ext, :] + s2_ref[PH - 2:n_ext - 2, :]
            else:
                s4_ref[16:n_ext, :] = s2_ref[16:n_ext, :] + s2_ref[14:n_ext - 2, :]
                if win == 8:
                    acc = s4_ref[PH:n_ext, :] + s4_ref[PH - 4:n_ext - 4, :]
                else:
                    s8 = s4_ref[24:n_ext, :] + s4_ref[20:n_ext - 4, :]
                    acc = s8[8:] + s8[:-8]
        inv_cnt = 1.0 / jnp.minimum(tpos, win).astype(F32)
        d = acc * inv_cnt - pg
        yg = jnp.dot(d.astype(BF16), poolw_ref[g], preferred_element_type=F32)
        y_ref[:, cs] = (yg * pscale_ref[:, cs]).astype(BF16)
    pext_ref[0:PH, :] = pext_ref[ts:ts + PH, :]

    nblk = W // dh
    for blk in range(2 * nblk):
        hs = slice(blk * dh, (blk + 1) * dh)
        cext_ref[CH:CH + ts, hs] = proj(W + blk * dh, dh)
        lo = CH - (CONV_W - 1)
        y = cext_ref[lo:lo + ts, hs] * conv_ref[0:1, hs]
        for j in range(1, CONV_W):
            y = y + cext_ref[lo + j:lo + j + ts, hs] * conv_ref[j:j + 1, hs]
        y = _silu(y)
        cext_ref[0:CH, hs] = cext_ref[ts:ts + CH, hs]
        if blk < nblk:
            qs_ref[:, hs] = (y * (dh ** -0.5)).astype(BF16)
        else:
            ks_ref[:, (blk - nblk) * dh:(blk - nblk + 1) * dh] = y

    gates = jnp.dot(xn_ref[...], wgate_ref[...], preferred_element_type=F32) + gbias_ref[...]
    li = gates[:, :LANES]
    lf = _log_sigmoid(gates[:, LANES:])
    b = _scan_rows(lf, jnp.add, 0.0, row)
    a = li - b
    m_prev = m_ref[...]
    u = jnp.maximum(m_prev, _scan_rows(a, jnp.maximum, -jnp.inf, row))
    u_l = u[ts - 1:ts, :]
    m_ref[...] = b[ts - 1:ts, :] + u_l
    iw = jnp.exp(m_prev - u)
    emt = jnp.exp(-b - u)
    ws = jnp.exp(a - u_l)
    decay = jnp.exp(m_prev - u_l)
    a_row = a.T
    tri = (lax.broadcasted_iota(jnp.int32, (ts, ts), 0)
           >= lax.broadcasted_iota(jnp.int32, (ts, ts), 1))

    for hd in range(H):
        cs = slice(hd * dh, (hd + 1) * dh)
        q_c = qs_ref[:, cs]
        k_c = ks_ref[:, cs]
        v_c = proj(3 * W + hd * dh, dh).astype(BF16)
        og = proj(4 * W + hd * dh, dh)
        sc = lax.dot_general(q_c, k_c.astype(BF16), (((1,), (1,)), ((), ())),
                             preferred_element_type=F32)
        e = jnp.where(tri, jnp.exp(a_row[hd:hd + 1, :] - u[:, hd:hd + 1]), 0.0)
        wm = e * sc
        cst = c_ref[hd]
        nst = n_ref[hd]
        iw_h = iw[:, hd:hd + 1]
        num = (iw_h * jnp.dot(q_c, cst.astype(BF16), preferred_element_type=F32)
               + jnp.dot(wm.astype(BF16), v_c, preferred_element_type=F32))
        den = (iw_h * jnp.sum(q_c.astype(F32) * nst, axis=-1, keepdims=True)
               + jnp.sum(wm, axis=-1, keepdims=True))
        hh = num * (1.0 / jnp.maximum(jnp.abs(den), emt[:, hd:hd + 1]))
        hh = hh * lax.rsqrt(jnp.mean(hh * hh, axis=-1, keepdims=True) + EPS)
        yv = hh * hn_ref[:, cs] * jax.nn.sigmoid(og)
        y_ref[:, W + hd * dh:W + (hd + 1) * dh] = yv.astype(BF16)
        kw = k_c * ws[:, hd:hd + 1]
        dec_h = decay[:, hd:hd + 1]
        c_ref[hd] = dec_h * cst + lax.dot_general(
            kw.astype(BF16), v_c, (((0,), (0,)), ((), ())), preferred_element_type=F32)
        n_ref[hd] = dec_h * nst + jnp.sum(kw, axis=0, keepdims=True)

    o_ref[...] = x + jnp.dot(y_ref[...], wout_ref[...], preferred_element_type=F32)


def _mix(h, g, w_main, w_gate, gbias, conv_w, head_norm, pool_w, pool_scale, w_out, layer, *, B, S):
    T, D = h.shape
    W = D // 2
    H = MLSTM_HEADS
    dh = W // H
    ng = len(POOL_WINDOWS)
    gw = W // ng
    ts = _tile(S, MIX_TILE)
    nst = S // ts
    return pl.pallas_call(
        functools.partial(_mix_kernel, ts=ts, W=W),
        grid=(B, nst),
        in_specs=[
            pl.BlockSpec((ts, D), lambda b, s: (b * nst + s, 0)),
            pl.BlockSpec((None, 1, D), lambda b, s: (layer, 0, 0)),
            pl.BlockSpec((D, 5 * W), lambda b, s: (0, 0), **_RESIDENT),
            pl.BlockSpec((None, D, 2 * LANES), lambda b, s: (layer, 0, 0), **_RESIDENT),
            pl.BlockSpec((None, 1, 2 * LANES), lambda b, s: (layer, 0, 0)),
            pl.BlockSpec((None, CONV_W, 2 * W), lambda b, s: (layer, 0, 0)),
            pl.BlockSpec((None, 1, W), lambda b, s: (layer, 0, 0)),
            pl.BlockSpec((ng, gw, gw), lambda b, s: (0, 0, 0)),
            pl.BlockSpec((None, 1, W), lambda b, s: (layer, 0, 0)),
            pl.BlockSpec((D, D), lambda b, s: (0, 0), **_RESIDENT),
        ],
        out_specs=pl.BlockSpec((ts, D), lambda b, s: (b * nst + s, 0)),
        out_shape=jax.ShapeDtypeStruct((T, D), F32),
        scratch_shapes=[
            pltpu.VMEM((POOL_HALO + ts, W), F32),
            pltpu.VMEM((POOL_HALO + ts, gw), F32),
            pltpu.VMEM((POOL_HALO + ts, gw), F32),
            pltpu.VMEM((CONV_HALO + ts, 2 * W), F32),
            pltpu.VMEM((H, dh, dh), F32),
            pltpu.VMEM((H, 1, dh), F32),
            pltpu.VMEM((1, LANES), F32),
            pltpu.VMEM((ts, D), BF16),
            pltpu.VMEM((ts, W), BF16),
            pltpu.VMEM((ts, W), F32),
            pltpu.VMEM((ts, D), BF16),
        ],
        compiler_params=_params(MIX_VMEM, "arbitrary", "arbitrary"),
        name="mix",
    )(h, g, w_main, w_gate, gbias, conv_w, head_norm, pool_w, pool_scale, w_out)


def _kv_kernel(m_ref, g_ref, w_ref, o_ref, xn_ref):
    j = pl.program_id(1)

    @pl.when(j == 0)
    def _():
        xn_ref[...] = _rms(m_ref[...], g_ref[...]).astype(BF16)

    o_ref[...] = jnp.dot(xn_ref[...], w_ref[...], preferred_element_type=F32).astype(BF16)


def _kv_proj(mem, g, wkv, layer):
    T, D = mem.shape
    N = wkv.shape[-1]
    tm = _tile(T, KV_TM)
    tn = _tile(N, KV_TN)
    return pl.pallas_call(
        _kv_kernel,
        grid=(T // tm, N // tn),
        in_specs=[
            pl.BlockSpec((tm, D), lambda i, j: (i, 0)),
            pl.BlockSpec((None, 1, D), lambda i, j: (layer, 0, 0)),
            pl.BlockSpec((D, tn), lambda i, j: (0, j)),
        ],
        out_specs=pl.BlockSpec((tm, tn), lambda i, j: (i, j)),
        out_shape=jax.ShapeDtypeStruct((T, N), BF16),
        scratch_shapes=[pltpu.VMEM((tm, D), BF16)],
        compiler_params=_params(KV_VMEM, "parallel", "arbitrary"),
        name="xattn_kv",
    )(mem, g, wkv)


def _xattn_kernel(*refs, D, nc):
    h_ref, g_ref, wq_ref, kv_ref, wo_ref = refs[:5]
    o_ref = refs[5 + nc]
    xn_ref, ao_ref = refs[6 + 2 * nc:]
    H = XATTN_HEADS
    dh = D // H
    x = h_ref[...]
    xn_ref[...] = _rms(x, g_ref[...]).astype(BF16)
    for hd in range(H):
        cs = slice(hd * dh, (hd + 1) * dh)
        q = jnp.dot(xn_ref[...], wq_ref[:, cs], preferred_element_type=F32)
        s = lax.dot_general(q.astype(BF16), kv_ref[:, cs], (((1,), (1,)), ((), ())),
                            preferred_element_type=F32) * (dh ** -0.5)
        e = jnp.exp(s - jnp.max(s, axis=-1, keepdims=True))
        pr = e * (1.0 / jnp.sum(e, axis=-1, keepdims=True))
        ao_ref[:, cs] = jnp.dot(pr.astype(BF16), kv_ref[:, D + hd * dh:D + (hd + 1) * dh],
                                preferred_element_type=F32).astype(BF16)
    o_ref[...] = x + jnp.dot(ao_ref[...], wo_ref[...], preferred_element_type=F32)
    _copy_casts(refs[5:5 + nc], refs[6 + nc:6 + 2 * nc])


def _xattn(h, g, wq, kv, wo, layer, casts, *, S, M):
    T, D = h.shape
    tm = _tile(S, XATTN_TM)
    nsb = S // tm
    grid = (T // tm,)
    jobs = [(_cast_job_t if t else _cast_job)(w, wl, grid) for w, wl, t in casts]
    outs = pl.pallas_call(
        functools.partial(_xattn_kernel, D=D, nc=len(jobs)),
        grid=grid,
        in_specs=[
            pl.BlockSpec((tm, D), lambda i: (i, 0)),
            pl.BlockSpec((None, 1, D), lambda i: (layer, 0, 0)),
            pl.BlockSpec((D, D), lambda i: (0, 0), **_RESIDENT),
            pl.BlockSpec((M, 2 * D), lambda i: (i // nsb, 0)),
            pl.BlockSpec((D, D), lambda i: (0, 0), **_RESIDENT),
        ] + [jb[0] for jb in jobs],
        out_specs=[pl.BlockSpec((tm, D), lambda i: (i, 0))] + [jb[1] for jb in jobs],
        out_shape=[jax.ShapeDtypeStruct((T, D), F32)] + [jb[2] for jb in jobs],
        scratch_shapes=[pltpu.VMEM((tm, D), BF16), pltpu.VMEM((tm, D), BF16)],
        compiler_params=_params(XATTN_VMEM, "parallel"),
        name="xattn",
    )(h, g, wq, kv, wo, *[c[0] for c in casts])
    return outs[0], list(outs[1:])


def kernel(x, mem, ffn1_norm, ffn1_w_gate, ffn1_w_up, ffn1_w_down, mix_norm, w_in, gate_bias, qk_conv, head_norm, pool_w, pool_scale, w_out, xattn_norm, mem_norm, xattn_wq, xattn_wkv, xattn_wo, ffn2_norm, ffn2_w_gate, ffn2_w_up, ffn2_w_down, final_norm):
    B, S, D = x.shape
    M = mem.shape[1]
    depth = ffn1_norm.shape[0]
    W = D // 2
    H = MLSTM_HEADS
    n_main = 5 * W

    bf = lambda w: w.astype(BF16)
    row = lambda v: v.reshape(v.shape[0], 1, v.shape[-1])

    lane_pad = lambda v: jnp.pad(v, [(0, 0)] * (v.ndim - 1) + [(0, LANES - H)])
    w_gate = bf(jnp.concatenate([lane_pad(w_in[:, :, n_main:n_main + H]),
                                 lane_pad(w_in[:, :, n_main + H:])], axis=-1))
    gbias = jnp.concatenate([lane_pad(gate_bias[:, :H]), lane_pad(gate_bias[:, H:])],
                            axis=-1).reshape(depth, 1, 2 * LANES)
    fg = final_norm.reshape(1, D)
    pool_rows = pool_w.reshape(depth, -1, pool_w.shape[-1])
    w_in_t = jnp.swapaxes(w_in, 1, 2)

    def ffn_w(wg, wu, wd, l):
        return [(wg, l, False), (wu, l, False), (wd, l, False)]

    def mix_xattn_w(l):
        return [(w_in_t, l, True)] + [(w, l, False) for w in (pool_rows, w_out, xattn_wq, xattn_wkv, xattn_wo)]

    fw = [bf(w[0]) for w, _, _ in ffn_w(ffn1_w_gate, ffn1_w_up, ffn1_w_down, 0)]
    mw = [bf(w_in[0])] + [bf(w[0]) for w, _, _ in mix_xattn_w(0)[1:]]
    h = x.reshape(B * S, D)
    mem2 = mem.reshape(B * M, D)
    for l in range(depth):
        last = l == depth - 1
        h, _ = _ffn(h, row(ffn1_norm), fw[0], fw[1], fw[2], fg, l, [], final=False)
        w_main, poolw, wout, wq, wkv, wo = mw
        h = _mix(h, row(mix_norm), w_main, w_gate, gbias, qk_conv, row(head_norm),
                 poolw.reshape(pool_w.shape[1:]), row(pool_scale), wout, l, B=B, S=S)
        kv = _kv_proj(mem2, row(mem_norm), wkv, l)
        casts = ffn_w(ffn2_w_gate, ffn2_w_up, ffn2_w_down, l) + ([] if last else mix_xattn_w(l + 1))
        h, cw = _xattn(h, row(xattn_norm), wq, kv, wo, l, casts, S=S, M=M)
        fw, mw = cw[:3], cw[3:]
        casts = [] if last else ffn_w(ffn1_w_gate, ffn1_w_up, ffn1_w_down, l + 1)
        h, cw = _ffn(h, row(ffn2_norm), fw[0], fw[1], fw[2], fg, l, casts, final=last)
        fw = cw
    return h.reshape(B, S, D)
```

```python
import functools

import jax
import jax.numpy as jnp
from jax import lax
from jax.experimental import pallas as pl
from jax.experimental.pallas import tpu as pltpu

F32 = jnp.float32
BF16 = jnp.bfloat16

EPS = 1e-6
MLSTM_HEADS = 4
POOL_WINDOWS = (2, 4, 8, 16)
CONV_W = 4
XATTN_HEADS = 4
LANES = 128
BF16_ROWS = 16
PANEL = 512
POOL_HALO = 32
CONV_HALO = 8
MIB = 1024 * 1024

FFN_TM, FFN_TF, FFN_VMEM = 1024, 512, 60 * MIB
MIX_TILE, MIX_VMEM = 256, 56 * MIB
XATTN_TM, XATTN_VMEM = 512, 58 * MIB
KV_TM, KV_TN, KV_VMEM = 512, 1024, 32 * MIB


def _tile(n, pref):
    return pref if n % pref == 0 else n


def _rms(x, g):
    y = x * lax.rsqrt(jnp.mean(x * x, axis=-1, keepdims=True) + EPS)
    return y * g


def _silu(x):
    h = 0.5 * x
    return h * jnp.tanh(h) + h


def _log_sigmoid(x):
    return jnp.minimum(x, 0.0) - jnp.log1p(jnp.exp(-jnp.abs(x)))


def _params(vmem, *sem):
    return pltpu.CompilerParams(dimension_semantics=sem, vmem_limit_bytes=vmem)


_RESIDENT = dict(pipeline_mode=pl.Buffered(1))


def _panel_cols(ref, lo, n):
    p, off = divmod(lo, PANEL)
    assert off + n <= PANEL
    return ref[p, :, off:off + n]


def _to_panels(w):
    K, N = w.shape
    return w.reshape(K, N // PANEL, PANEL).transpose(1, 0, 2)


def _cast_job(w, layer, grid, panels=False):
    _, R, C = w.shape
    ni = grid[0]
    rep = 1
    while (R * rep) % (ni * BF16_ROWS) and rep < ni:
        rep *= 2
    nrb = ni // rep
    assert ni % rep == 0 and R % nrb == 0
    ncol = 1
    if len(grid) == 2:
        assert not panels
        ncol = max([n for n in range(2, grid[1] + 1) if C % (n * LANES) == 0] + [1])
        idx = lambda i, j: (i // rep, jnp.minimum(j, ncol - 1))
    else:
        idx = lambda i: (i // rep, 0)
    src = pl.BlockSpec((None, R // nrb, C // ncol), lambda *ij: (layer,) + idx(*ij))
    if panels:
        assert C % PANEL == 0
        dst = pl.BlockSpec((C // PANEL, R // nrb, PANEL), lambda i: (0, i // rep, 0))
        return src, dst, jax.ShapeDtypeStruct((C // PANEL, R, PANEL), BF16), "panels"
    dst = pl.BlockSpec((R // nrb, C // ncol), idx)
    return src, dst, jax.ShapeDtypeStruct((R, C), BF16), "plain"


def _cast_job_t(wt, layer, grid, ncols):
    R = wt.shape[-1]
    assert ncols % PANEL == 0
    per = pl.cdiv(ncols // LANES, grid[0])
    lpp = PANEL // LANES
    per = min(d for d in (1, 2, 4) if d >= per) if per <= lpp else lpp * pl.cdiv(per, lpp)
    assert lpp == 4
    cpb = per * LANES
    last = pl.cdiv(ncols, cpb) - 1
    src = pl.BlockSpec((None, cpb, R), lambda i: (layer, jnp.minimum(i, last), 0))
    if cpb <= PANEL:
        bpp = PANEL // cpb
        dst = pl.BlockSpec((None, R, cpb), lambda i: (jnp.minimum(i, last) // bpp, 0,
                                                      jnp.minimum(i, last) % bpp))
    else:
        dst = pl.BlockSpec((cpb // PANEL, R, PANEL), lambda i: (jnp.minimum(i, last), 0, 0))
    return src, dst, jax.ShapeDtypeStruct((ncols // PANEL, R, PANEL), BF16), "transposed"


def _copy_casts(src_refs, dst_refs, kinds):
    for src_ref, dst_ref, kind in zip(src_refs, dst_refs, kinds):
        v = src_ref[...]
        if kind == "transposed":
            v = v.T
        if dst_ref.ndim == 2:
            dst_ref[...] = v.astype(BF16)
        else:
            for p in range(dst_ref.shape[0]):
                dst_ref[p] = v[:, p * PANEL:(p + 1) * PANEL].astype(BF16)


def _ffn_kernel(*refs, final, kinds):
    nc = len(kinds)
    h_ref, g_ref, wg_ref, wu_ref, wd_ref, fg_ref = refs[:6]
    src_refs = refs[6:6 + nc]
    o_ref = refs[6 + nc]
    dst_refs = refs[7 + nc:7 + 2 * nc]
    xn_ref = refs[7 + 2 * nc]
    j = pl.program_id(1)

    @pl.when(j == 0)
    def _():
        x = h_ref[...]
        xn_ref[...] = _rms(x, g_ref[...]).astype(BF16)
        o_ref[...] = x

    xn = xn_ref[...]
    a = jnp.dot(xn, wg_ref[...], preferred_element_type=F32)
    b = jnp.dot(xn, wu_ref[...], preferred_element_type=F32)
    mid = (_silu(a) * b).astype(BF16)
    o_ref[...] += 0.5 * jnp.dot(mid, wd_ref[...], preferred_element_type=F32)

    _copy_casts(src_refs, dst_refs, kinds)

    if final:
        @pl.when(j == pl.num_programs(1) - 1)
        def _():
            o_ref[...] = _rms(o_ref[...], fg_ref[...])


def _ffn(h, g, wg, wu, wd, fg, layer, casts, *, final):
    T, D = h.shape
    F = wg.shape[-1]
    tm = _tile(T, FFN_TM)
    tf = _tile(F, FFN_TF)
    ni, nj = T // tm, F // tf
    jobs = [_cast_job(w, wl, (ni, nj)) for w, wl in casts]
    outs = pl.pallas_call(
        functools.partial(_ffn_kernel, final=final, kinds=tuple(jb[3] for jb in jobs)),
        grid=(ni, nj),
        in_specs=[
            pl.BlockSpec((tm, D), lambda i, j: (i, 0)),
            pl.BlockSpec((None, 1, D), lambda i, j: (layer, 0, 0)),
            pl.BlockSpec((D, tf), lambda i, j: (0, j)),
            pl.BlockSpec((D, tf), lambda i, j: (0, j)),
            pl.BlockSpec((tf, D), lambda i, j: (j, 0)),
            pl.BlockSpec((1, D), lambda i, j: (0, 0)),
        ] + [jb[0] for jb in jobs],
        out_specs=[pl.BlockSpec((tm, D), lambda i, j: (i, 0))] + [jb[1] for jb in jobs],
        out_shape=[jax.ShapeDtypeStruct((T, D), F32)] + [jb[2] for jb in jobs],
        scratch_shapes=[pltpu.VMEM((tm, D), BF16)],
        compiler_params=_params(FFN_VMEM, "arbitrary", "arbitrary"),
        name="ffn_final" if final else "ffn",
    )(h, g, wg, wu, wd, fg, *[c[0] for c in casts])
    return outs[0], list(outs[1:])


def _scan_rows(x, op, fill, row):
    n = x.shape[0]
    sh = 1
    while sh < n:
        x = op(x, jnp.where(row >= sh, pltpu.roll(x, sh, axis=0), fill))
        sh *= 2
    return x


def _mix_kernel(h_ref, g_ref, win_ref, wgate_ref, gbias_ref, conv_ref, hn_ref, poolw_ref,
                pscale_ref, wout_ref,
                o_ref,
                pext_ref, s2_ref, s4_ref, cext_ref, c_ref, n_ref, m_ref,
                xn_ref, qs_ref, ks_ref, y_ref,
                *, ts, W):
    H = MLSTM_HEADS
    dh = W // H
    gw = W // len(POOL_WINDOWS)
    pw = min(PANEL, W)
    PH, CH = POOL_HALO, CONV_HALO
    s = pl.program_id(1)

    @pl.when(s == 0)
    def _():
        pext_ref[0:PH, :] = jnp.zeros((PH, W), F32)
        cext_ref[0:CH, :] = jnp.zeros((CH, 2 * W), F32)
        c_ref[...] = jnp.zeros_like(c_ref)
        n_ref[...] = jnp.zeros_like(n_ref)
        m_ref[...] = jnp.zeros_like(m_ref)

    x = h_ref[...]
    xn_ref[...] = _rms(x, g_ref[...]).astype(BF16)
    row = lax.broadcasted_iota(jnp.int32, (ts, 1), 0)

    def proj(lo, n):
        return jnp.dot(xn_ref[...], _panel_cols(win_ref, lo, n), preferred_element_type=F32)

    for c0 in range(0, W, pw):
        pext_ref[PH:PH + ts, c0:c0 + pw] = proj(c0, pw)
    tpos = row + s * ts + 1
    n_ext = PH + ts
    for g, win in enumerate(POOL_WINDOWS):
        cs = slice(g * gw, (g + 1) * gw)
        pg = pext_ref[PH:n_ext, cs]
        if win == 2:
            acc = pg + pext_ref[PH - 1:n_ext - 1, cs]
        else:
            s2_ref[8:n_ext, :] = pext_ref[8:n_ext, cs] + pext_ref[7:n_ext - 1, cs]
            if win == 4:
                acc = s2_ref[PH:n_ext, :] + s2_ref[PH - 2:n_ext - 2, :]
            else:
                s4_ref[16:n_ext, :] = s2_ref[16:n_ext, :] + s2_ref[14:n_ext - 2, :]
                if win == 8:
                    acc = s4_ref[PH:n_ext, :] + s4_ref[PH - 4:n_ext - 4, :]
                else:
                    s8 = s4_ref[24:n_ext, :] + s4_ref[20:n_ext - 4, :]
                    acc = s8[8:] + s8[:-8]
        inv_cnt = 1.0 / jnp.minimum(tpos, win).astype(F32)
        d = acc * inv_cnt - pg
        yg = jnp.dot(d.astype(BF16), poolw_ref[g], preferred_element_type=F32)
        y_ref[:, cs] = (yg * pscale_ref[:, cs]).astype(BF16)
    pext_ref[0:PH, :] = pext_ref[ts:ts + PH, :]

    nblk = W // dh
    for c0 in range(0, 2 * W, pw):
        cext_ref[CH:CH + ts, c0:c0 + pw] = proj(W + c0, pw)
    for blk in range(2 * nblk):
        hs = slice(blk * dh, (blk + 1) * dh)
        lo = CH - (CONV_W - 1)
        y = cext_ref[lo:lo + ts, hs] * conv_ref[0:1, hs]
        for j in range(1, CONV_W):
            y = y + cext_ref[lo + j:lo + j + ts, hs] * conv_ref[j:j + 1, hs]
        y = _silu(y)
        cext_ref[0:CH, hs] = cext_ref[ts:ts + CH, hs]
        if blk < nblk:
            qs_ref[:, hs] = (y * (dh ** -0.5)).astype(BF16)
        else:
            ks_ref[:, (blk - nblk) * dh:(blk - nblk + 1) * dh] = y

    gates = jnp.dot(xn_ref[...], wgate_ref[...], preferred_element_type=F32) + gbias_ref[...]
    li = gates[:, :LANES]
    lf = _log_sigmoid(gates[:, LANES:])
    b = _scan_rows(lf, jnp.add, 0.0, row)
    a = li - b
    m_prev = m_ref[...]
    u = jnp.maximum(m_prev, _scan_rows(a, jnp.maximum, -jnp.inf, row))
    u_l = u[ts - 1:ts, :]
    m_ref[...] = b[ts - 1:ts, :] + u_l
    iw = jnp.exp(m_prev - u)
    emt = jnp.exp(-b - u)
    ws = jnp.exp(a - u_l)
    decay = jnp.exp(m_prev - u_l)
    a_row = a.T
    tri = (lax.broadcasted_iota(jnp.int32, (ts, ts), 0)
           >= lax.broadcasted_iota(jnp.int32, (ts, ts), 1))

    for hd in range(H):
        cs = slice(hd * dh, (hd + 1) * dh)
        q_c = qs_ref[:, cs]
        k_c = ks_ref[:, cs]
        v_c = proj(3 * W + hd * dh, dh).astype(BF16)
        og = proj(4 * W + hd * dh, dh)
        sc = lax.dot_general(q_c, k_c.astype(BF16), (((1,), (1,)), ((), ())),
                             preferred_element_type=F32)
        e = jnp.where(tri, jnp.exp(a_row[hd:hd + 1, :] - u[:, hd:hd + 1]), 0.0)
        wm = e * sc
        cst = c_ref[hd]
        nst = n_ref[hd]
        iw_h = iw[:, hd:hd + 1]
        num = (iw_h * jnp.dot(q_c, cst.astype(BF16), preferred_element_type=F32)
               + jnp.dot(wm.astype(BF16), v_c, preferred_element_type=F32))
        den = (iw_h * jnp.sum(q_c.astype(F32) * nst, axis=-1, keepdims=True)
               + jnp.sum(wm, axis=-1, keepdims=True))
        hh = num * (1.0 / jnp.maximum(jnp.abs(den), emt[:, hd:hd + 1]))
        hh = hh * lax.rsqrt(jnp.mean(hh * hh, axis=-1, keepdims=True) + EPS)
        yv = hh * hn_ref[:, cs] * jax.nn.sigmoid(og)
        y_ref[:, W + hd * dh:W + (hd + 1) * dh] = yv.astype(BF16)
        kw = k_c * ws[:, hd:hd + 1]
        dec_h = decay[:, hd:hd + 1]
        c_ref[hd] = dec_h * cst + lax.dot_general(
            kw.astype(BF16), v_c, (((0,), (0,)), ((), ())), preferred_element_type=F32)
        n_ref[hd] = dec_h * nst + jnp.sum(kw, axis=0, keepdims=True)

    for p in range(wout_ref.shape[0]):
        cs = slice(p * PANEL, (p + 1) * PANEL)
        o_ref[:, cs] = x[:, cs] + jnp.dot(y_ref[...], wout_ref[p], preferred_element_type=F32)


def _mix(h, g, w_main, w_gate, gbias, conv_w, head_norm, pool_w, pool_scale, w_out, layer, *, B, S):
    T, D = h.shape
    W = D // 2
    H = MLSTM_HEADS
    dh = W // H
    ng = len(POOL_WINDOWS)
    gw = W // ng
    ts = _tile(S, MIX_TILE)
    nst = S // ts
    return pl.pallas_call(
        functools.partial(_mix_kernel, ts=ts, W=W),
        grid=(B, nst),
        in_specs=[
            pl.BlockSpec((ts, D), lambda b, s: (b * nst + s, 0)),
            pl.BlockSpec((None, 1, D), lambda b, s: (layer, 0, 0)),
            pl.BlockSpec(w_main.shape, lambda b, s: (0, 0, 0), **_RESIDENT),
            pl.BlockSpec((None, D, 2 * LANES), lambda b, s: (layer, 0, 0), **_RESIDENT),
            pl.BlockSpec((None, 1, 2 * LANES), lambda b, s: (layer, 0, 0)),
            pl.BlockSpec((None, CONV_W, 2 * W), lambda b, s: (layer, 0, 0)),
            pl.BlockSpec((None, 1, W), lambda b, s: (layer, 0, 0)),
            pl.BlockSpec((ng, gw, gw), lambda b, s: (0, 0, 0)),
            pl.BlockSpec((None, 1, W), lambda b, s: (layer, 0, 0)),
            pl.BlockSpec(w_out.shape, lambda b, s: (0, 0, 0), **_RESIDENT),
        ],
        out_specs=pl.BlockSpec((ts, D), lambda b, s: (b * nst + s, 0)),
        out_shape=jax.ShapeDtypeStruct((T, D), F32),
        scratch_shapes=[
            pltpu.VMEM((POOL_HALO + ts, W), F32),
            pltpu.VMEM((POOL_HALO + ts, gw), F32),
            pltpu.VMEM((POOL_HALO + ts, gw), F32),
            pltpu.VMEM((CONV_HALO + ts, 2 * W), F32),
            pltpu.VMEM((H, dh, dh), F32),
            pltpu.VMEM((H, 1, dh), F32),
            pltpu.VMEM((1, LANES), F32),
            pltpu.VMEM((ts, D), BF16),
            pltpu.VMEM((ts, W), BF16),
            pltpu.VMEM((ts, W), F32),
            pltpu.VMEM((ts, D), BF16),
        ],
        compiler_params=_params(MIX_VMEM, "arbitrary", "arbitrary"),
        name="mix",
    )(h, g, w_main, w_gate, gbias, conv_w, head_norm, pool_w, pool_scale, w_out)


def _kv_kernel(m_ref, g_ref, w_ref, o_ref, xn_ref):
    j = pl.program_id(1)

    @pl.when(j == 0)
    def _():
        xn_ref[...] = _rms(m_ref[...], g_ref[...]).astype(BF16)

    o_ref[...] = jnp.dot(xn_ref[...], w_ref[...], preferred_element_type=F32).astype(BF16)


def _kv_proj(mem, g, wkv, layer):
    T, D = mem.shape
    N = wkv.shape[-1]
    tm = _tile(T, KV_TM)
    tn = _tile(N, KV_TN)
    return pl.pallas_call(
        _kv_kernel,
        grid=(T // tm, N // tn),
        in_specs=[
            pl.BlockSpec((tm, D), lambda i, j: (i, 0)),
            pl.BlockSpec((None, 1, D), lambda i, j: (layer, 0, 0)),
            pl.BlockSpec((D, tn), lambda i, j: (0, j)),
        ],
        out_specs=pl.BlockSpec((tm, tn), lambda i, j: (i, j)),
        out_shape=jax.ShapeDtypeStruct((T, N), BF16),
        scratch_shapes=[pltpu.VMEM((tm, D), BF16)],
        compiler_params=_params(KV_VMEM, "parallel", "arbitrary"),
        name="xattn_kv",
    )(mem, g, wkv)


def _xattn_kernel(*refs, D, kinds):
    nc = len(kinds)
    h_ref, g_ref, wq_ref, kv_ref, wo_ref = refs[:5]
    o_ref = refs[5 + nc]
    xn_ref, ao_ref = refs[6 + 2 * nc:]
    H = XATTN_HEADS
    dh = D // H
    x = h_ref[...]
    xn_ref[...] = _rms(x, g_ref[...]).astype(BF16)
    for hd in range(H):
        cs = slice(hd * dh, (hd + 1) * dh)
        q = jnp.dot(xn_ref[...], _panel_cols(wq_ref, hd * dh, dh), preferred_element_type=F32)
        s = lax.dot_general(q.astype(BF16), kv_ref[:, cs], (((1,), (1,)), ((), ())),
                            preferred_element_type=F32) * (dh ** -0.5)
        e = jnp.exp(s - jnp.max(s, axis=-1, keepdims=True))
        pr = e * (1.0 / jnp.sum(e, axis=-1, keepdims=True))
        ao_ref[:, cs] = jnp.dot(pr.astype(BF16), kv_ref[:, D + hd * dh:D + (hd + 1) * dh],
                                preferred_element_type=F32).astype(BF16)
    for p in range(wo_ref.shape[0]):
        cs = slice(p * PANEL, (p + 1) * PANEL)
        o_ref[:, cs] = x[:, cs] + jnp.dot(ao_ref[...], wo_ref[p], preferred_element_type=F32)
    _copy_casts(refs[5:5 + nc], refs[6 + nc:6 + 2 * nc], kinds)


def _xattn(h, g, wq, kv, wo, layer, jobs, srcs, *, S, M):
    T, D = h.shape
    tm = _tile(S, XATTN_TM)
    nsb = S // tm
    outs = pl.pallas_call(
        functools.partial(_xattn_kernel, D=D, kinds=tuple(jb[3] for jb in jobs)),
        grid=(T // tm,),
        in_specs=[
            pl.BlockSpec((tm, D), lambda i: (i, 0)),
            pl.BlockSpec((None, 1, D), lambda i: (layer, 0, 0)),
            pl.BlockSpec(wq.shape, lambda i: (0, 0, 0), **_RESIDENT),
            pl.BlockSpec((M, 2 * D), lambda i: (i // nsb, 0)),
            pl.BlockSpec(wo.shape, lambda i: (0, 0, 0), **_RESIDENT),
        ] + [jb[0] for jb in jobs],
        out_specs=[pl.BlockSpec((tm, D), lambda i: (i, 0))] + [jb[1] for jb in jobs],
        out_shape=[jax.ShapeDtypeStruct((T, D), F32)] + [jb[2] for jb in jobs],
        scratch_shapes=[pltpu.VMEM((tm, D), BF16), pltpu.VMEM((tm, D), BF16)],
        compiler_params=_params(XATTN_VMEM, "arbitrary"),
        name="xattn",
    )(h, g, wq, kv, wo, *srcs)
    return outs[0], list(outs[1:])


def kernel(x, mem, ffn1_norm, ffn1_w_gate, ffn1_w_up, ffn1_w_down, mix_norm, w_in, gate_bias, qk_conv, head_norm, pool_w, pool_scale, w_out, xattn_norm, mem_norm, xattn_wq, xattn_wkv, xattn_wo, ffn2_norm, ffn2_w_gate, ffn2_w_up, ffn2_w_down, final_norm):
    B, S, D = x.shape
    M = mem.shape[1]
    depth = ffn1_norm.shape[0]
    W = D // 2
    H = MLSTM_HEADS
    n_main = 5 * W

    bf = lambda w: w.astype(BF16)
    row = lambda v: v.reshape(v.shape[0], 1, v.shape[-1])

    w_in_t = jnp.swapaxes(w_in, 1, 2)
    lane_pad = lambda v: jnp.pad(v, [(0, 0)] * (v.ndim - 1) + [(0, LANES - H)])
    gate_cols = jnp.swapaxes(w_in_t[:, n_main:, :], 1, 2)
    w_gate = bf(jnp.concatenate([lane_pad(gate_cols[..., :H]), lane_pad(gate_cols[..., H:])], axis=-1))
    gbias = jnp.concatenate([lane_pad(gate_bias[:, :H]), lane_pad(gate_bias[:, H:])],
                            axis=-1).reshape(depth, 1, 2 * LANES)
    fg = final_norm.reshape(1, D)
    pool_rows = pool_w.reshape(depth, -1, pool_w.shape[-1])

    def ffn_w(wg, wu, wd, l):
        return [(wg, l), (wu, l), (wd, l)]

    xgrid = (B * S // _tile(S, XATTN_TM),)

    def xattn_jobs(l, last):
        srcs = [ffn2_w_gate, ffn2_w_up, ffn2_w_down]
        jobs = [_cast_job(w, l, xgrid) for w in srcs]
        if not last:
            srcs += [w_in_t, pool_rows, w_out, xattn_wq, xattn_wkv, xattn_wo]
            jobs += [_cast_job_t(w_in_t, l + 1, xgrid, n_main), _cast_job(pool_rows, l + 1, xgrid),
                     _cast_job(w_out, l + 1, xgrid, panels=True),
                     _cast_job(xattn_wq, l + 1, xgrid, panels=True),
                     _cast_job(xattn_wkv, l + 1, xgrid),
                     _cast_job(xattn_wo, l + 1, xgrid, panels=True)]
        return jobs, srcs

    fw = [bf(ffn1_w_gate[0]), bf(ffn1_w_up[0]), bf(ffn1_w_down[0])]
    w_main0 = bf(w_in_t[0, :n_main]).reshape(n_main // PANEL, PANEL, D).transpose(0, 2, 1)
    mw = [w_main0, bf(pool_rows[0]), _to_panels(bf(w_out[0])), _to_panels(bf(xattn_wq[0])),
          bf(xattn_wkv[0]), _to_panels(bf(xattn_wo[0]))]
    h = x.reshape(B * S, D)
    mem2 = mem.reshape(B * M, D)
    for l in range(depth):
        last = l == depth - 1
        h, _ = _ffn(h, row(ffn1_norm), fw[0], fw[1], fw[2], fg, l, [], final=False)
        w_main, poolw, wout, wq, wkv, wo = mw
        h = _mix(h, row(mix_norm), w_main, w_gate, gbias, qk_conv, row(head_norm),
                 poolw.reshape(pool_w.shape[1:]), row(pool_scale), wout, l, B=B, S=S)
        kv = _kv_proj(mem2, row(mem_norm), wkv, l)
        jobs, srcs = xattn_jobs(l, last)
        h, cw = _xattn(h, row(xattn_norm), wq, kv, wo, l, jobs, srcs, S=S, M=M)
        fw, mw = cw[:3], cw[3:]
        casts = [] if last else ffn_w(ffn1_w_gate, ffn1_w_up, ffn1_w_down, l + 1)
        h, cw = _ffn(h, row(ffn2_norm), fw[0], fw[1], fw[2], fg, l, casts, final=last)
        fw = cw
    return h.reshape(B, S, D)
```

```python
import functools

import jax
import jax.numpy as jnp
from jax import lax
from jax.experimental import pallas as pl
from jax.experimental.pallas import tpu as pltpu

F32 = jnp.float32
BF16 = jnp.bfloat16

EPS = 1e-6
MLSTM_HEADS = 4
POOL_WINDOWS = (2, 4, 8, 16)
CONV_W = 4
XATTN_HEADS = 4
LANES = 128
BF16_ROWS = 16
PANEL = 512
POOL_HALO = 32
CONV_HALO = 8
MIB = 1024 * 1024

FFN_TM, FFN_TF, FFN_VMEM = 1024, 512, 60 * MIB
MIX_TILE, MIX_VMEM = 256, 56 * MIB
XATTN_TM, XATTN_VMEM = 512, 58 * MIB
KV_TM, KV_TN, KV_VMEM = 512, 1024, 32 * MIB


def _tile(n, pref):
    return pref if n % pref == 0 else n


def _rms(x, g):
    y = x * lax.rsqrt(jnp.mean(x * x, axis=-1, keepdims=True) + EPS)
    return y * g


def _silu(x):
    h = 0.5 * x
    return h * jnp.tanh(h) + h


def _log_sigmoid(x):
    return jnp.minimum(x, 0.0) - jnp.log1p(jnp.exp(-jnp.abs(x)))


def _params(vmem, *sem):
    return pltpu.CompilerParams(dimension_semantics=sem, vmem_limit_bytes=vmem)


_RESIDENT = dict(pipeline_mode=pl.Buffered(1))


def _panel_cols(ref, lo, n):
    p, off = divmod(lo, PANEL)
    assert off + n <= PANEL
    return ref[p, :, off:off + n]


def _to_panels(w):
    K, N = w.shape
    return w.reshape(K, N // PANEL, PANEL).transpose(1, 0, 2)


def _cast_job(w, layer, grid, panels=False):
    _, R, C = w.shape
    ni = grid[0]
    rep = 1
    while (R * rep) % (ni * BF16_ROWS) and rep < ni:
        rep *= 2
    nrb = ni // rep
    assert ni % rep == 0 and R % nrb == 0
    ncol = 1
    if len(grid) == 2:
        assert not panels
        ncol = max([n for n in range(2, grid[1] + 1) if C % (n * LANES) == 0] + [1])
        idx = lambda i, j: (i // rep, jnp.minimum(j, ncol - 1))
    else:
        idx = lambda i: (i // rep, 0)
    src = pl.BlockSpec((None, R // nrb, C // ncol), lambda *ij: (layer,) + idx(*ij))
    if panels:
        assert C % PANEL == 0
        dst = pl.BlockSpec((C // PANEL, R // nrb, PANEL), lambda i: (0, i // rep, 0))
        return src, dst, jax.ShapeDtypeStruct((C // PANEL, R, PANEL), BF16), "panels"
    dst = pl.BlockSpec((R // nrb, C // ncol), idx)
    return src, dst, jax.ShapeDtypeStruct((R, C), BF16), "plain"


def _cast_job_t(wt, layer, grid, ncols):
    R = wt.shape[-1]
    assert ncols % PANEL == 0
    per = pl.cdiv(ncols // LANES, grid[0])
    lpp = PANEL // LANES
    per = min(d for d in (1, 2, 4) if d >= per) if per <= lpp else lpp * pl.cdiv(per, lpp)
    assert lpp == 4
    cpb = per * LANES
    last = pl.cdiv(ncols, cpb) - 1
    src = pl.BlockSpec((None, cpb, R), lambda i: (layer, jnp.minimum(i, last), 0))
    if cpb <= PANEL:
        bpp = PANEL // cpb
        dst = pl.BlockSpec((None, R, cpb), lambda i: (jnp.minimum(i, last) // bpp, 0,
                                                      jnp.minimum(i, last) % bpp))
    else:
        dst = pl.BlockSpec((cpb // PANEL, R, PANEL), lambda i: (jnp.minimum(i, last), 0, 0))
    return src, dst, jax.ShapeDtypeStruct((ncols // PANEL, R, PANEL), BF16), "transposed"


def _copy_casts(src_refs, dst_refs, kinds):
    for src_ref, dst_ref, kind in zip(src_refs, dst_refs, kinds):
        v = src_ref[...]
        if kind == "transposed":
            v = v.T
        if dst_ref.ndim == 2:
            dst_ref[...] = v.astype(BF16)
        else:
            for p in range(dst_ref.shape[0]):
                dst_ref[p] = v[:, p * PANEL:(p + 1) * PANEL].astype(BF16)


def _ffn_kernel(*refs, final, kinds):
    nc = len(kinds)
    h_ref, g_ref, wg_ref, wu_ref, wd_ref, fg_ref = refs[:6]
    src_refs = refs[6:6 + nc]
    o_ref = refs[6 + nc]
    dst_refs = refs[7 + nc:7 + 2 * nc]
    xn_ref = refs[7 + 2 * nc]
    j = pl.program_id(1)

    @pl.when(j == 0)
    def _():
        x = h_ref[...]
        xn_ref[...] = _rms(x, g_ref[...]).astype(BF16)
        o_ref[...] = x

    xn = xn_ref[...]
    a = jnp.dot(xn, wg_ref[...], preferred_element_type=F32)
    b = jnp.dot(xn, wu_ref[...], preferred_element_type=F32)
    mid = (_silu(a) * b).astype(BF16)
    o_ref[...] += 0.5 * jnp.dot(mid, wd_ref[...], preferred_element_type=F32)

    _copy_casts(src_refs, dst_refs, kinds)

    if final:
        @pl.when(j == pl.num_programs(1) - 1)
        def _():
            o_ref[...] = _rms(o_ref[...], fg_ref[...])


def _ffn(h, g, wg, wu, wd, fg, layer, casts, *, final):
    T, D = h.shape
    F = wg.shape[-1]
    tm = _tile(T, FFN_TM)
    tf = _tile(F, FFN_TF)
    ni, nj = T // tm, F // tf
    jobs = [_cast_job(w, wl, (ni, nj)) for w, wl in casts]
    outs = pl.pallas_call(
        functools.partial(_ffn_kernel, final=final, kinds=tuple(jb[3] for jb in jobs)),
        grid=(ni, nj),
        in_specs=[
            pl.BlockSpec((tm, D), lambda i, j: (i, 0)),
            pl.BlockSpec((None, 1, D), lambda i, j: (layer, 0, 0)),
            pl.BlockSpec((D, tf), lambda i, j: (0, j)),
            pl.BlockSpec((D, tf), lambda i, j: (0, j)),
            pl.BlockSpec((tf, D), lambda i, j: (j, 0)),
            pl.BlockSpec((1, D), lambda i, j: (0, 0)),
        ] + [jb[0] for jb in jobs],
        out_specs=[pl.BlockSpec((tm, D), lambda i, j: (i, 0))] + [jb[1] for jb in jobs],
        out_shape=[jax.ShapeDtypeStruct((T, D), F32)] + [jb[2] for jb in jobs],
        scratch_shapes=[pltpu.VMEM((tm, D), BF16)],
        compiler_params=_params(FFN_VMEM, "arbitrary", "arbitrary"),
        name="ffn_final" if final else "ffn",
    )(h, g, wg, wu, wd, fg, *[c[0] for c in casts])
    return outs[0], list(outs[1:])


def _scan_rows(x, op, fill, row):
    n = x.shape[0]
    sh = 1
    while sh < n:
        x = op(x, jnp.where(row >= sh, pltpu.roll(x, sh, axis=0), fill))
        sh *= 2
    return x


def _mix_kernel(h_ref, g_ref, win_ref, wgate_ref, gbias_ref, conv_ref, hn_ref, poolw_ref,
                pscale_ref, wout_ref,
                o_ref,
                pext_ref, s2_ref, s4_ref, cext_ref, c_ref, n_ref, m_ref,
                xn_ref, qs_ref, ks_ref, y_ref,
                *, ts, W):
    H = MLSTM_HEADS
    dh = W // H
    gw = W // len(POOL_WINDOWS)
    pw = min(PANEL, W)
    PH, CH = POOL_HALO, CONV_HALO
    s = pl.program_id(1)

    @pl.when(s == 0)
    def _():
        pext_ref[0:PH, :] = jnp.zeros((PH, W), F32)
        cext_ref[0:CH, :] = jnp.zeros((CH, 2 * W), F32)
        c_ref[...] = jnp.zeros_like(c_ref)
        n_ref[...] = jnp.zeros_like(n_ref)
        m_ref[...] = jnp.zeros_like(m_ref)

    x = h_ref[...]
    xn_ref[...] = _rms(x, g_ref[...]).astype(BF16)
    row = lax.broadcasted_iota(jnp.int32, (ts, 1), 0)

    def proj(lo, n):
        return jnp.dot(xn_ref[...], _panel_cols(win_ref, lo, n), preferred_element_type=F32)

    for c0 in range(0, W, pw):
        pext_ref[PH:PH + ts, c0:c0 + pw] = proj(c0, pw)
    tpos = row + s * ts + 1
    n_ext = PH + ts
    for g, win in enumerate(POOL_WINDOWS):
        cs = slice(g * gw, (g + 1) * gw)
        pg = pext_ref[PH:n_ext, cs]
        if win == 2:
            acc = pg + pext_ref[PH - 1:n_ext - 1, cs]
        else:
            s2_ref[8:n_ext, :] = pext_ref[8:n_ext, cs] + pext_ref[7:n_ext - 1, cs]
            if win == 4:
                acc = s2_ref[PH:n_ext, :] + s2_ref[PH - 2:n_ext - 2, :]
            else:
                s4_ref[16:n_ext, :] = s2_ref[16:n_ext, :] + s2_ref[14:n_ext - 2, :]
                if win == 8:
                    acc = s4_ref[PH:n_ext, :] + s4_ref[PH - 4:n_ext - 4, :]
                else:
                    s8 = s4_ref[24:n_ext, :] + s4_ref[20:n_ext - 4, :]
                    acc = s8[8:] + s8[:-8]
        inv_cnt = 1.0 / jnp.minimum(tpos, win).astype(F32)
        d = acc * inv_cnt - pg
        yg = jnp.dot(d.astype(BF16), poolw_ref[g], preferred_element_type=F32)
        y_ref[:, cs] = (yg * pscale_ref[:, cs]).astype(BF16)
    pext_ref[0:PH, :] = pext_ref[ts:ts + PH, :]

    nblk = W // dh
    for c0 in range(0, 2 * W, pw):
        cext_ref[CH:CH + ts, c0:c0 + pw] = proj(W + c0, pw)
    for blk in range(2 * nblk):
        hs = slice(blk * dh, (blk + 1) * dh)
        lo = CH - (CONV_W - 1)
        y = cext_ref[lo:lo + ts, hs] * conv_ref[0:1, hs]
        for j in range(1, CONV_W):
            y = y + cext_ref[lo + j:lo + j + ts, hs] * conv_ref[j:j + 1, hs]
        y = _silu(y)
        cext_ref[0:CH, hs] = cext_ref[ts:ts + CH, hs]
        if blk < nblk:
            qs_ref[:, hs] = (y * (dh ** -0.5)).astype(BF16)
        else:
            ks_ref[:, (blk - nblk) * dh:(blk - nblk + 1) * dh] = y

    gates = jnp.dot(xn_ref[...], wgate_ref[...], preferred_element_type=F32) + gbias_ref[...]
    li = gates[:, :LANES]
    lf = _log_sigmoid(gates[:, LANES:])
    b = _scan_rows(lf, jnp.add, 0.0, row)
    a = li - b
    m_prev = m_ref[...]
    u = jnp.maximum(m_prev, _scan_rows(a, jnp.maximum, -jnp.inf, row))
    u_l = u[ts - 1:ts, :]
    m_ref[...] = b[ts - 1:ts, :] + u_l
    iw = jnp.exp(m_prev - u)
    emt = jnp.exp(-b - u)
    ws = jnp.exp(a - u_l)
    decay = jnp.exp(m_prev - u_l)
    a_row = a.T
    tri = (lax.broadcasted_iota(jnp.int32, (ts, ts), 0)
           >= lax.broadcasted_iota(jnp.int32, (ts, ts), 1))

    for hd in range(H):
        cs = slice(hd * dh, (hd + 1) * dh)
        q_c = qs_ref[:, cs]
        k_c = ks_ref[:, cs]
        v_c = proj(3 * W + hd * dh, dh).astype(BF16)
        og = proj(4 * W + hd * dh, dh)
        sc = lax.dot_general(q_c, k_c.astype(BF16), (((1,), (1,)), ((), ())),
                             preferred_element_type=F32)
        e = jnp.where(tri, jnp.exp(a_row[hd:hd + 1, :] - u[:, hd:hd + 1]), 0.0)
        wm = e * sc
        cst = c_ref[hd]
        nst = n_ref[hd]
        iw_h = iw[:, hd:hd + 1]
        num = (iw_h * jnp.dot(q_c, cst.astype(BF16), preferred_element_type=F32)
               + jnp.dot(wm.astype(BF16), v_c, preferred_element_type=F32))
        den = (iw_h * jnp.sum(q_c.astype(F32) * nst, axis=-1, keepdims=True)
               + jnp.sum(wm, axis=-1, keepdims=True))
        hh = num * (1.0 / jnp.maximum(jnp.abs(den), emt[:, hd:hd + 1]))
        hh = hh * lax.rsqrt(jnp.mean(hh * hh, axis=-1, keepdims=True) + EPS)
        yv = hh * hn_ref[:, cs] * jax.nn.sigmoid(og)
        y_ref[:, W + hd * dh:W + (hd + 1) * dh] = yv.astype(BF16)
        kw = k_c * ws[:, hd:hd + 1]
        dec_h = decay[:, hd:hd + 1]
        c_ref[hd] = dec_h * cst + lax.dot_general(
            kw.astype(BF16), v_c, (((0,), (0,)), ((), ())), preferred_element_type=F32)
        n_ref[hd] = dec_h * nst + jnp.sum(kw, axis=0, keepdims=True)

    for p in range(wout_ref.shape[0]):
        cs = slice(p * PANEL, (p + 1) * PANEL)
        o_ref[:, cs] = x[:, cs] + jnp.dot(y_ref[...], wout_ref[p], preferred_element_type=F32)


def _mix(h, g, w_main, w_gate, gbias, conv_w, head_norm, pool_w, pool_scale, w_out, layer, *, B, S):
    T, D = h.shape
    W = D // 2
    H = MLSTM_HEADS
    dh = W // H
    ng = len(POOL_WINDOWS)
    gw = W // ng
    ts = _tile(S, MIX_TILE)
    nst = S // ts
    return pl.pallas_call(
        functools.partial(_mix_kernel, ts=ts, W=W),
        grid=(B, nst),
        in_specs=[
            pl.BlockSpec((ts, D), lambda b, s: (b * nst + s, 0)),
            pl.BlockSpec((None, 1, D), lambda b, s: (layer, 0, 0)),
            pl.BlockSpec(w_main.shape, lambda b, s: (0, 0, 0), **_RESIDENT),
            pl.BlockSpec((None, D, 2 * LANES), lambda b, s: (layer, 0, 0), **_RESIDENT),
            pl.BlockSpec((None, 1, 2 * LANES), lambda b, s: (layer, 0, 0)),
            pl.BlockSpec((None, CONV_W, 2 * W), lambda b, s: (layer, 0, 0)),
            pl.BlockSpec((None, 1, W), lambda b, s: (layer, 0, 0)),
            pl.BlockSpec((ng, gw, gw), lambda b, s: (0, 0, 0)),
            pl.BlockSpec((None, 1, W), lambda b, s: (layer, 0, 0)),
            pl.BlockSpec(w_out.shape, lambda b, s: (0, 0, 0), **_RESIDENT),
        ],
        out_specs=pl.BlockSpec((ts, D), lambda b, s: (b * nst + s, 0)),
        out_shape=jax.ShapeDtypeStruct((T, D), F32),
        scratch_shapes=[
            pltpu.VMEM((POOL_HALO + ts, W), F32),
            pltpu.VMEM((POOL_HALO + ts, gw), F32),
            pltpu.VMEM((POOL_HALO + ts, gw), F32),
            pltpu.VMEM((CONV_HALO + ts, 2 * W), F32),
            pltpu.VMEM((H, dh, dh), F32),
            pltpu.VMEM((H, 1, dh), F32),
            pltpu.VMEM((1, LANES), F32),
            pltpu.VMEM((ts, D), BF16),
            pltpu.VMEM((ts, W), BF16),
            pltpu.VMEM((ts, W), F32),
            pltpu.VMEM((ts, D), BF16),
        ],
        compiler_params=_params(MIX_VMEM, "arbitrary", "arbitrary"),
        name="mix",
    )(h, g, w_main, w_gate, gbias, conv_w, head_norm, pool_w, pool_scale, w_out)


def _kv_kernel(m_ref, g_ref, w_ref, o_ref, xn_ref):
    j = pl.program_id(1)

    @pl.when(j == 0)
    def _():
        xn_ref[...] = _rms(m_ref[...], g_ref[...]).astype(BF16)

    o_ref[...] = jnp.dot(xn_ref[...], w_ref[...], preferred_element_type=F32).astype(BF16)


def _kv_proj(mem, g, wkv, layer):
    T, D = mem.shape
    N = wkv.shape[-1]
    tm = _tile(T, KV_TM)
    tn = _tile(N, KV_TN)
    return pl.pallas_call(
        _kv_kernel,
        grid=(T // tm, N // tn),
        in_specs=[
            pl.BlockSpec((tm, D), lambda i, j: (i, 0)),
            pl.BlockSpec((None, 1, D), lambda i, j: (layer, 0, 0)),
            pl.BlockSpec((D, tn), lambda i, j: (0, j)),
        ],
        out_specs=pl.BlockSpec((tm, tn), lambda i, j: (i, j)),
        out_shape=jax.ShapeDtypeStruct((T, N), BF16),
        scratch_shapes=[pltpu.VMEM((tm, D), BF16)],
        compiler_params=_params(KV_VMEM, "parallel", "arbitrary"),
        name="xattn_kv",
    )(mem, g, wkv)


def _xattn_kernel(*refs, D, kinds):
    nc = len(kinds)
    h_ref, g_ref, wq_ref, kv_ref, wo_ref = refs[:5]
    o_ref = refs[5 + nc]
    xn_ref, ao_ref = refs[6 + 2 * nc:]
    H = XATTN_HEADS
    dh = D // H
    x = h_ref[...]
    xn_ref[...] = _rms(x, g_ref[...]).astype(BF16)
    for hd in range(H):
        cs = slice(hd * dh, (hd + 1) * dh)
        q = jnp.dot(xn_ref[...], _panel_cols(wq_ref, hd * dh, dh), preferred_element_type=F32)
        s = lax.dot_general(kv_ref[:, cs], q.astype(BF16), (((1,), (1,)), ((), ())),
                            preferred_element_type=F32) * (dh ** -0.5)
        e = jnp.exp(s - jnp.max(s, axis=0, keepdims=True))
        pr = e * (1.0 / jnp.sum(e, axis=0, keepdims=True))
        ao_ref[:, cs] = lax.dot_general(pr.astype(BF16), kv_ref[:, D + hd * dh:D + (hd + 1) * dh],
                                        (((0,), (0,)), ((), ())),
                                        preferred_element_type=F32).astype(BF16)
    for p in range(wo_ref.shape[0]):
        cs = slice(p * PANEL, (p + 1) * PANEL)
        o_ref[:, cs] = x[:, cs] + jnp.dot(ao_ref[...], wo_ref[p], preferred_element_type=F32)
    _copy_casts(refs[5:5 + nc], refs[6 + nc:6 + 2 * nc], kinds)


def _xattn(h, g, wq, kv, wo, layer, jobs, srcs, *, S, M):
    T, D = h.shape
    tm = _tile(S, XATTN_TM)
    nsb = S // tm
    outs = pl.pallas_call(
        functools.partial(_xattn_kernel, D=D, kinds=tuple(jb[3] for jb in jobs)),
        grid=(T // tm,),
        in_specs=[
            pl.BlockSpec((tm, D), lambda i: (i, 0)),
            pl.BlockSpec((None, 1, D), lambda i: (layer, 0, 0)),
            pl.BlockSpec(wq.shape, lambda i: (0, 0, 0), **_RESIDENT),
            pl.BlockSpec((M, 2 * D), lambda i: (i // nsb, 0)),
            pl.BlockSpec(wo.shape, lambda i: (0, 0, 0), **_RESIDENT),
        ] + [jb[0] for jb in jobs],
        out_specs=[pl.BlockSpec((tm, D), lambda i: (i, 0))] + [jb[1] for jb in jobs],
        out_shape=[jax.ShapeDtypeStruct((T, D), F32)] + [jb[2] for jb in jobs],
        scratch_shapes=[pltpu.VMEM((tm, D), BF16), pltpu.VMEM((tm, D), BF16)],
        compiler_params=_params(XATTN_VMEM, "arbitrary"),
        name="xattn",
    )(h, g, wq, kv, wo, *srcs)
    return outs[0], list(outs[1:])


def kernel(x, mem, ffn1_norm, ffn1_w_gate, ffn1_w_up, ffn1_w_down, mix_norm, w_in, gate_bias, qk_conv, head_norm, pool_w, pool_scale, w_out, xattn_norm, mem_norm, xattn_wq, xattn_wkv, xattn_wo, ffn2_norm, ffn2_w_gate, ffn2_w_up, ffn2_w_down, final_norm):
    B, S, D = x.shape
    M = mem.shape[1]
    depth = ffn1_norm.shape[0]
    W = D // 2
    H = MLSTM_HEADS
    n_main = 5 * W

    bf = lambda w: w.astype(BF16)
    row = lambda v: v.reshape(v.shape[0], 1, v.shape[-1])

    w_in_t = jnp.swapaxes(w_in, 1, 2)
    lane_pad = lambda v: jnp.pad(v, [(0, 0)] * (v.ndim - 1) + [(0, LANES - H)])
    gate_cols = jnp.swapaxes(w_in_t[:, n_main:, :], 1, 2)
    w_gate = bf(jnp.concatenate([lane_pad(gate_cols[..., :H]), lane_pad(gate_cols[..., H:])], axis=-1))
    gbias = jnp.concatenate([lane_pad(gate_bias[:, :H]), lane_pad(gate_bias[:, H:])],
                            axis=-1).reshape(depth, 1, 2 * LANES)
    fg = final_norm.reshape(1, D)
    pool_rows = pool_w.reshape(depth, -1, pool_w.shape[-1])

    def ffn_w(wg, wu, wd, l):
        return [(wg, l), (wu, l), (wd, l)]

    xgrid = (B * S // _tile(S, XATTN_TM),)

    def xattn_jobs(l, last):
        srcs = [ffn2_w_gate, ffn2_w_up, ffn2_w_down]
        jobs = [_cast_job(w, l, xgrid) for w in srcs]
        if not last:
            srcs += [w_in_t, pool_rows, w_out, xattn_wq, xattn_wkv, xattn_wo]
            jobs += [_cast_job_t(w_in_t, l + 1, xgrid, n_main), _cast_job(pool_rows, l + 1, xgrid),
                     _cast_job(w_out, l + 1, xgrid, panels=True),
                     _cast_job(xattn_wq, l + 1, xgrid, panels=True),
                     _cast_job(xattn_wkv, l + 1, xgrid),
                     _cast_job(xattn_wo, l + 1, xgrid, panels=True)]
        return jobs, srcs

    fw = [bf(ffn1_w_gate[0]), bf(ffn1_w_up[0]), bf(ffn1_w_down[0])]
    w_main0 = bf(lax.optimization_barrier(w_in_t[0, :n_main]))
    w_main0 = w_main0.reshape(n_main // PANEL, PANEL, D).transpose(0, 2, 1)
    mw = [w_main0, bf(pool_rows[0]), _to_panels(bf(w_out[0])), _to_panels(bf(xattn_wq[0])),
          bf(xattn_wkv[0]), _to_panels(bf(xattn_wo[0]))]
    h = x.reshape(B * S, D)
    mem2 = mem.reshape(B * M, D)
    for l in range(depth):
        last = l == depth - 1
        h, _ = _ffn(h, row(ffn1_norm), fw[0], fw[1], fw[2], fg, l, [], final=False)
        w_main, poolw, wout, wq, wkv, wo = mw
        h = _mix(h, row(mix_norm), w_main, w_gate, gbias, qk_conv, row(head_norm),
                 poolw.reshape(pool_w.shape[1:]), row(pool_scale), wout, l, B=B, S=S)
        kv = _kv_proj(mem2, row(mem_norm), wkv, l)
        jobs, srcs = xattn_jobs(l, last)
        h, cw = _xattn(h, row(xattn_norm), wq, kv, wo, l, jobs, srcs, S=S, M=M)
        fw, mw = cw[:3], cw[3:]
        casts = [] if last else ffn_w(ffn1_w_gate, ffn1_w_up, ffn1_w_down, l + 1)
        h, cw = _ffn(h, row(ffn2_norm), fw[0], fw[1], fw[2], fg, l, casts, final=last)
        fw = cw
    return h.reshape(B, S, D)
```

```python
import functools

import jax
import jax.numpy as jnp
from jax import lax
from jax.experimental import pallas as pl
from jax.experimental.pallas import tpu as pltpu

F32 = jnp.float32
BF16 = jnp.bfloat16

EPS = 1e-6
MLSTM_HEADS = 4
POOL_WINDOWS = (2, 4, 8, 16)
CONV_W = 4
XATTN_HEADS = 4
LANES = 128
BF16_ROWS = 16
PANEL = 512
POOL_HALO = 32
CONV_HALO = 8
MIB = 1024 * 1024

FFN_TM, FFN_TF, FFN_VMEM = 1024, 512, 60 * MIB
MIX_TILE, MIX_VMEM = 256, 56 * MIB
XATTN_TM, XATTN_VMEM = 512, 58 * MIB
KV_TM, KV_TN, KV_VMEM = 512, 1024, 32 * MIB


def _tile(n, pref):
    return pref if n % pref == 0 else n


def _rms(x, g):
    y = x * lax.rsqrt(jnp.mean(x * x, axis=-1, keepdims=True) + EPS)
    return y * g


def _silu(x):
    h = 0.5 * x
    return h * jnp.tanh(h) + h


def _log_sigmoid(x):
    return jnp.minimum(x, 0.0) - jnp.log1p(jnp.exp(-jnp.abs(x)))


def _params(vmem, *sem):
    return pltpu.CompilerParams(dimension_semantics=sem, vmem_limit_bytes=vmem)


_RESIDENT = dict(pipeline_mode=pl.Buffered(1))


def _panel_cols(ref, lo, n):
    p, off = divmod(lo, PANEL)
    assert off + n <= PANEL
    return ref[p, :, off:off + n]


def _to_panels(w):
    K, N = w.shape
    return w.reshape(K, N // PANEL, PANEL).transpose(1, 0, 2)


def _cast_job(w, layer, grid, panels=False):
    _, R, C = w.shape
    ni = grid[0]
    rep = 1
    while (R * rep) % (ni * BF16_ROWS) and rep < ni:
        rep *= 2
    nrb = ni // rep
    assert ni % rep == 0 and R % nrb == 0
    ncol = 1
    if len(grid) == 2:
        assert not panels
        ncol = max([n for n in range(2, grid[1] + 1) if C % (n * LANES) == 0] + [1])
        idx = lambda i, j: (i // rep, jnp.minimum(j, ncol - 1))
    else:
        idx = lambda i: (i // rep, 0)
    src = pl.BlockSpec((None, R // nrb, C // ncol), lambda *ij: (layer,) + idx(*ij))
    if panels:
        assert C % PANEL == 0
        dst = pl.BlockSpec((C // PANEL, R // nrb, PANEL), lambda i: (0, i // rep, 0))
        return src, dst, jax.ShapeDtypeStruct((C // PANEL, R, PANEL), BF16), "panels"
    dst = pl.BlockSpec((R // nrb, C // ncol), idx)
    return src, dst, jax.ShapeDtypeStruct((R, C), BF16), "plain"


def _cast_job_t(wt, layer, grid, ncols):
    R = wt.shape[-1]
    assert ncols % PANEL == 0
    per = pl.cdiv(ncols // LANES, grid[0])
    lpp = PANEL // LANES
    per = min(d for d in (1, 2, 4) if d >= per) if per <= lpp else lpp * pl.cdiv(per, lpp)
    assert lpp == 4
    cpb = per * LANES
    last = pl.cdiv(ncols, cpb) - 1
    blk = lambda *ij: jnp.minimum(ij[0], last)
    src = pl.BlockSpec((None, cpb, R), lambda *ij: (layer, blk(*ij), 0))
    if cpb <= PANEL:
        bpp = PANEL // cpb
        dst = pl.BlockSpec((None, R, cpb), lambda *ij: (blk(*ij) // bpp, 0, blk(*ij) % bpp))
    else:
        dst = pl.BlockSpec((cpb // PANEL, R, PANEL), lambda *ij: (blk(*ij), 0, 0))
    return src, dst, jax.ShapeDtypeStruct((ncols // PANEL, R, PANEL), BF16), "transposed"


def _copy_casts(src_refs, dst_refs, kinds):
    for src_ref, dst_ref, kind in zip(src_refs, dst_refs, kinds):
        v = src_ref[...]
        if kind == "transposed":
            v = v.T
        if dst_ref.ndim == 2:
            dst_ref[...] = v.astype(BF16)
        else:
            for p in range(dst_ref.shape[0]):
                dst_ref[p] = v[:, p * PANEL:(p + 1) * PANEL].astype(BF16)


def _copy_call(job, src, n_steps):
    def body(src_ref, dst_ref):
        _copy_casts([src_ref], [dst_ref], [job[3]])

    return pl.pallas_call(
        body, grid=(n_steps,), in_specs=[job[0]], out_specs=job[1], out_shape=job[2],
        compiler_params=_params(KV_VMEM, "arbitrary"), name="weight_copy")(src)


def _ffn_kernel(*refs, final, kinds):
    nc = len(kinds)
    h_ref, g_ref, wg_ref, wu_ref, wd_ref, fg_ref = refs[:6]
    src_refs = refs[6:6 + nc]
    o_ref = refs[6 + nc]
    dst_refs = refs[7 + nc:7 + 2 * nc]
    xn_ref = refs[7 + 2 * nc]
    j = pl.program_id(1)

    @pl.when(j == 0)
    def _():
        x = h_ref[...]
        xn_ref[...] = _rms(x, g_ref[...]).astype(BF16)
        o_ref[...] = x

    xn = xn_ref[...]
    a = jnp.dot(xn, wg_ref[...], preferred_element_type=F32)
    b = jnp.dot(xn, wu_ref[...], preferred_element_type=F32)
    mid = (_silu(a) * b).astype(BF16)
    o_ref[...] += 0.5 * jnp.dot(mid, wd_ref[...], preferred_element_type=F32)

    _copy_casts(src_refs, dst_refs, kinds)

    if final:
        @pl.when(j == pl.num_programs(1) - 1)
        def _():
            o_ref[...] = _rms(o_ref[...], fg_ref[...])


def _ffn_grid(T, F):
    tm, tf = _tile(T, FFN_TM), _tile(F, FFN_TF)
    return T // tm, F // tf


def _ffn(h, g, wg, wu, wd, fg, layer, jobs, srcs, *, final):
    T, D = h.shape
    F = wg.shape[-1]
    ni, nj = _ffn_grid(T, F)
    tm, tf = T // ni, F // nj
    outs = pl.pallas_call(
        functools.partial(_ffn_kernel, final=final, kinds=tuple(jb[3] for jb in jobs)),
        grid=(ni, nj),
        in_specs=[
            pl.BlockSpec((tm, D), lambda i, j: (i, 0)),
            pl.BlockSpec((None, 1, D), lambda i, j: (layer, 0, 0)),
            pl.BlockSpec((D, tf), lambda i, j: (0, j)),
            pl.BlockSpec((D, tf), lambda i, j: (0, j)),
            pl.BlockSpec((tf, D), lambda i, j: (j, 0)),
            pl.BlockSpec((1, D), lambda i, j: (0, 0)),
        ] + [jb[0] for jb in jobs],
        out_specs=[pl.BlockSpec((tm, D), lambda i, j: (i, 0))] + [jb[1] for jb in jobs],
        out_shape=[jax.ShapeDtypeStruct((T, D), F32)] + [jb[2] for jb in jobs],
        scratch_shapes=[pltpu.VMEM((tm, D), BF16)],
        compiler_params=_params(FFN_VMEM, "arbitrary", "arbitrary"),
        name="ffn_final" if final else "ffn",
    )(h, g, wg, wu, wd, fg, *srcs)
    return outs[0], list(outs[1:])


def _scan_rows(x, op, fill, row):
    n = x.shape[0]
    sh = 1
    while sh < n:
        x = op(x, jnp.where(row >= sh, pltpu.roll(x, sh, axis=0), fill))
        sh *= 2
    return x


def _mix_kernel(h_ref, g_ref, win_ref, wgate_ref, gbias_ref, conv_ref, hn_ref, poolw_ref,
                pscale_ref, wout_ref,
                o_ref,
                pext_ref, s2_ref, s4_ref, cext_ref, c_ref, n_ref, m_ref,
                xn_ref, qs_ref, ks_ref, y_ref,
                *, ts, W):
    H = MLSTM_HEADS
    dh = W // H
    gw = W // len(POOL_WINDOWS)
    pw = min(PANEL, W)
    PH, CH = POOL_HALO, CONV_HALO
    s = pl.program_id(1)

    @pl.when(s == 0)
    def _():
        pext_ref[0:PH, :] = jnp.zeros((PH, W), F32)
        cext_ref[0:CH, :] = jnp.zeros((CH, 2 * W), F32)
        c_ref[...] = jnp.zeros_like(c_ref)
        n_ref[...] = jnp.zeros_like(n_ref)
        m_ref[...] = jnp.zeros_like(m_ref)

    x = h_ref[...]
    xn_ref[...] = _rms(x, g_ref[...]).astype(BF16)
    row = lax.broadcasted_iota(jnp.int32, (ts, 1), 0)

    def proj(lo, n):
        return jnp.dot(xn_ref[...], _panel_cols(win_ref, lo, n), preferred_element_type=F32)

    for c0 in range(0, W, pw):
        pext_ref[PH:PH + ts, c0:c0 + pw] = proj(c0, pw)
    tpos = row + s * ts + 1
    n_ext = PH + ts
    for g, win in enumerate(POOL_WINDOWS):
        cs = slice(g * gw, (g + 1) * gw)
        pg = pext_ref[PH:n_ext, cs]
        if win == 2:
            acc = pg + pext_ref[PH - 1:n_ext - 1, cs]
        else:
            s2_ref[8:n_ext, :] = pext_ref[8:n_ext, cs] + pext_ref[7:n_ext - 1, cs]
            if win == 4:
                acc = s2_ref[PH:n_ext, :] + s2_ref[PH - 2:n_ext - 2, :]
            else:
                s4_ref[16:n_ext, :] = s2_ref[16:n_ext, :] + s2_ref[14:n_ext - 2, :]
                if win == 8:
                    acc = s4_ref[PH:n_ext, :] + s4_ref[PH - 4:n_ext - 4, :]
                else:
                    s8 = s4_ref[24:n_ext, :] + s4_ref[20:n_ext - 4, :]
                    acc = s8[8:] + s8[:-8]
        inv_cnt = 1.0 / jnp.minimum(tpos, win).astype(F32)
        d = acc * inv_cnt - pg
        yg = jnp.dot(d.astype(BF16), poolw_ref[g], preferred_element_type=F32)
        y_ref[:, cs] = (yg * pscale_ref[:, cs]).astype(BF16)
    pext_ref[0:PH, :] = pext_ref[ts:ts + PH, :]

    nblk = W // dh
    for c0 in range(0, 2 * W, pw):
        cext_ref[CH:CH + ts, c0:c0 + pw] = proj(W + c0, pw)
    for blk in range(2 * nblk):
        hs = slice(blk * dh, (blk + 1) * dh)
        lo = CH - (CONV_W - 1)
        y = cext_ref[lo:lo + ts, hs] * conv_ref[0:1, hs]
        for j in range(1, CONV_W):
            y = y + cext_ref[lo + j:lo + j + ts, hs] * conv_ref[j:j + 1, hs]
        y = _silu(y)
        cext_ref[0:CH, hs] = cext_ref[ts:ts + CH, hs]
        if blk < nblk:
            qs_ref[:, hs] = (y * (dh ** -0.5)).astype(BF16)
        else:
            ks_ref[:, (blk - nblk) * dh:(blk - nblk + 1) * dh] = y

    gates = jnp.dot(xn_ref[...], wgate_ref[...], preferred_element_type=F32) + gbias_ref[...]
    li = gates[:, :LANES]
    lf = _log_sigmoid(gates[:, LANES:])
    b = _scan_rows(lf, jnp.add, 0.0, row)
    a = li - b
    m_prev = m_ref[...]
    u = jnp.maximum(m_prev, _scan_rows(a, jnp.maximum, -jnp.inf, row))
    u_l = u[ts - 1:ts, :]
    m_ref[...] = b[ts - 1:ts, :] + u_l
    iw = jnp.exp(m_prev - u)
    emt = jnp.exp(-b - u)
    ws = jnp.exp(a - u_l)
    decay = jnp.exp(m_prev - u_l)
    a_row = a.T
    tri = (lax.broadcasted_iota(jnp.int32, (ts, ts), 0)
           >= lax.broadcasted_iota(jnp.int32, (ts, ts), 1))

    for hd in range(H):
        cs = slice(hd * dh, (hd + 1) * dh)
        q_c = qs_ref[:, cs]
        k_c = ks_ref[:, cs]
        v_c = proj(3 * W + hd * dh, dh).astype(BF16)
        og = proj(4 * W + hd * dh, dh)
        sc = lax.dot_general(q_c, k_c.astype(BF16), (((1,), (1,)), ((), ())),
                             preferred_element_type=F32)
        e = jnp.where(tri, jnp.exp(a_row[hd:hd + 1, :] - u[:, hd:hd + 1]), 0.0)
        wm = e * sc
        cst = c_ref[hd]
        nst = n_ref[hd]
        iw_h = iw[:, hd:hd + 1]
        num = (iw_h * jnp.dot(q_c, cst.astype(BF16), preferred_element_type=F32)
               + jnp.dot(wm.astype(BF16), v_c, preferred_element_type=F32))
        den = (iw_h * jnp.sum(q_c.astype(F32) * nst, axis=-1, keepdims=True)
               + jnp.sum(wm, axis=-1, keepdims=True))
        hh = num * (1.0 / jnp.maximum(jnp.abs(den), emt[:, hd:hd + 1]))
        hh = hh * lax.rsqrt(jnp.mean(hh * hh, axis=-1, keepdims=True) + EPS)
        yv = hh * hn_ref[:, cs] * jax.nn.sigmoid(og)
        y_ref[:, W + hd * dh:W + (hd + 1) * dh] = yv.astype(BF16)
        kw = k_c * ws[:, hd:hd + 1]
        dec_h = decay[:, hd:hd + 1]
        c_ref[hd] = dec_h * cst + lax.dot_general(
            kw.astype(BF16), v_c, (((0,), (0,)), ((), ())), preferred_element_type=F32)
        n_ref[hd] = dec_h * nst + jnp.sum(kw, axis=0, keepdims=True)

    for p in range(wout_ref.shape[0]):
        cs = slice(p * PANEL, (p + 1) * PANEL)
        o_ref[:, cs] = x[:, cs] + jnp.dot(y_ref[...], wout_ref[p], preferred_element_type=F32)


def _mix(h, g, w_main, w_gate, gbias, conv_w, head_norm, pool_w, pool_scale, w_out, layer, *, B, S):
    T, D = h.shape
    W = D // 2
    H = MLSTM_HEADS
    dh = W // H
    ng = len(POOL_WINDOWS)
    gw = W // ng
    ts = _tile(S, MIX_TILE)
    nst = S // ts
    return pl.pallas_call(
        functools.partial(_mix_kernel, ts=ts, W=W),
        grid=(B, nst),
        in_specs=[
            pl.BlockSpec((ts, D), lambda b, s: (b * nst + s, 0)),
            pl.BlockSpec((None, 1, D), lambda b, s: (layer, 0, 0)),
            pl.BlockSpec(w_main.shape, lambda b, s: (0, 0, 0), **_RESIDENT),
            pl.BlockSpec((None, D, 2 * LANES), lambda b, s: (layer, 0, 0), **_RESIDENT),
            pl.BlockSpec((None, 1, 2 * LANES), lambda b, s: (layer, 0, 0)),
            pl.BlockSpec((None, CONV_W, 2 * W), lambda b, s: (layer, 0, 0)),
            pl.BlockSpec((None, 1, W), lambda b, s: (layer, 0, 0)),
            pl.BlockSpec((ng, gw, gw), lambda b, s: (0, 0, 0)),
            pl.BlockSpec((None, 1, W), lambda b, s: (layer, 0, 0)),
            pl.BlockSpec(w_out.shape, lambda b, s: (0, 0, 0), **_RESIDENT),
        ],
        out_specs=pl.BlockSpec((ts, D), lambda b, s: (b * nst + s, 0)),
        out_shape=jax.ShapeDtypeStruct((T, D), F32),
        scratch_shapes=[
            pltpu.VMEM((POOL_HALO + ts, W), F32),
            pltpu.VMEM((POOL_HALO + ts, gw), F32),
            pltpu.VMEM((POOL_HALO + ts, gw), F32),
            pltpu.VMEM((CONV_HALO + ts, 2 * W), F32),
            pltpu.VMEM((H, dh, dh), F32),
            pltpu.VMEM((H, 1, dh), F32),
            pltpu.VMEM((1, LANES), F32),
            pltpu.VMEM((ts, D), BF16),
            pltpu.VMEM((ts, W), BF16),
            pltpu.VMEM((ts, W), F32),
            pltpu.VMEM((ts, D), BF16),
        ],
        compiler_params=_params(MIX_VMEM, "arbitrary", "arbitrary"),
        name="mix",
    )(h, g, w_main, w_gate, gbias, conv_w, head_norm, pool_w, pool_scale, w_out)


def _kv_kernel(m_ref, g_ref, w_ref, o_ref, xn_ref):
    j = pl.program_id(1)

    @pl.when(j == 0)
    def _():
        xn_ref[...] = _rms(m_ref[...], g_ref[...]).astype(BF16)

    o_ref[...] = jnp.dot(xn_ref[...], w_ref[...], preferred_element_type=F32).astype(BF16)


def _kv_proj(mem, g, wkv, layer):
    T, D = mem.shape
    N = wkv.shape[-1]
    tm = _tile(T, KV_TM)
    tn = _tile(N, KV_TN)
    return pl.pallas_call(
        _kv_kernel,
        grid=(T // tm, N // tn),
        in_specs=[
            pl.BlockSpec((tm, D), lambda i, j: (i, 0)),
            pl.BlockSpec((None, 1, D), lambda i, j: (layer, 0, 0)),
            pl.BlockSpec((D, tn), lambda i, j: (0, j)),
        ],
        out_specs=pl.BlockSpec((tm, tn), lambda i, j: (i, j)),
        out_shape=jax.ShapeDtypeStruct((T, N), BF16),
        scratch_shapes=[pltpu.VMEM((tm, D), BF16)],
        compiler_params=_params(KV_VMEM, "parallel", "arbitrary"),
        name="xattn_kv",
    )(mem, g, wkv)


def _xattn_kernel(*refs, D, kinds):
    nc = len(kinds)
    h_ref, g_ref, wq_ref, kv_ref, wo_ref = refs[:5]
    o_ref = refs[5 + nc]
    xn_ref, ao_ref = refs[6 + 2 * nc:]
    H = XATTN_HEADS
    dh = D // H
    x = h_ref[...]
    xn_ref[...] = _rms(x, g_ref[...]).astype(BF16)
    for hd in range(H):
        cs = slice(hd * dh, (hd + 1) * dh)
        q = jnp.dot(xn_ref[...], _panel_cols(wq_ref, hd * dh, dh), preferred_element_type=F32)
        s = lax.dot_general(q.astype(BF16), kv_ref[:, cs], (((1,), (1,)), ((), ())),
                            preferred_element_type=F32) * (dh ** -0.5)
        e = jnp.exp(s - jnp.max(s, axis=-1, keepdims=True))
        pr = e * (1.0 / jnp.sum(e, axis=-1, keepdims=True))
        ao_ref[:, cs] = jnp.dot(pr.astype(BF16), kv_ref[:, D + hd * dh:D + (hd + 1) * dh],
                                preferred_element_type=F32).astype(BF16)
    for p in range(wo_ref.shape[0]):
        cs = slice(p * PANEL, (p + 1) * PANEL)
        o_ref[:, cs] = x[:, cs] + jnp.dot(ao_ref[...], wo_ref[p], preferred_element_type=F32)
    _copy_casts(refs[5:5 + nc], refs[6 + nc:6 + 2 * nc], kinds)


def _xattn(h, g, wq, kv, wo, layer, jobs, srcs, *, S, M):
    T, D = h.shape
    tm = _tile(S, XATTN_TM)
    nsb = S // tm
    outs = pl.pallas_call(
        functools.partial(_xattn_kernel, D=D, kinds=tuple(jb[3] for jb in jobs)),
        grid=(T // tm,),
        in_specs=[
            pl.BlockSpec((tm, D), lambda i: (i, 0)),
            pl.BlockSpec((None, 1, D), lambda i: (layer, 0, 0)),
            pl.BlockSpec(wq.shape, lambda i: (0, 0, 0), **_RESIDENT),
            pl.BlockSpec((M, 2 * D), lambda i: (i // nsb, 0)),
            pl.BlockSpec(wo.shape, lambda i: (0, 0, 0), **_RESIDENT),
        ] + [jb[0] for jb in jobs],
        out_specs=[pl.BlockSpec((tm, D), lambda i: (i, 0))] + [jb[1] for jb in jobs],
        out_shape=[jax.ShapeDtypeStruct((T, D), F32)] + [jb[2] for jb in jobs],
        scratch_shapes=[pltpu.VMEM((tm, D), BF16), pltpu.VMEM((tm, D), BF16)],
        compiler_params=_params(XATTN_VMEM, "arbitrary"),
        name="xattn",
    )(h, g, wq, kv, wo, *srcs)
    return outs[0], list(outs[1:])


def kernel(x, mem, ffn1_norm, ffn1_w_gate, ffn1_w_up, ffn1_w_down, mix_norm, w_in, gate_bias, qk_conv, head_norm, pool_w, pool_scale, w_out, xattn_norm, mem_norm, xattn_wq, xattn_wkv, xattn_wo, ffn2_norm, ffn2_w_gate, ffn2_w_up, ffn2_w_down, final_norm):
    B, S, D = x.shape
    M = mem.shape[1]
    depth = ffn1_norm.shape[0]
    W = D // 2
    H = MLSTM_HEADS
    n_main = 5 * W

    bf = lambda w: w.astype(BF16)
    row = lambda v: v.reshape(v.shape[0], 1, v.shape[-1])

    w_in_t = jnp.swapaxes(w_in, 1, 2)
    lane_pad = lambda v: jnp.pad(v, [(0, 0)] * (v.ndim - 1) + [(0, LANES - H)])
    gate_cols = jnp.swapaxes(w_in_t[:, n_main:, :], 1, 2)
    w_gate = bf(jnp.concatenate([lane_pad(gate_cols[..., :H]), lane_pad(gate_cols[..., H:])], axis=-1))
    gbias = jnp.concatenate([lane_pad(gate_bias[:, :H]), lane_pad(gate_bias[:, H:])],
                            axis=-1).reshape(depth, 1, 2 * LANES)
    fg = final_norm.reshape(1, D)
    pool_rows = pool_w.reshape(depth, -1, pool_w.shape[-1])

    fgrid = _ffn_grid(B * S, ffn1_w_gate.shape[-1])

    def ffn_jobs(ws, l):
        return [_cast_job(w, l, fgrid) for w in ws], list(ws)

    xgrid = (B * S // _tile(S, XATTN_TM),)

    def xattn_jobs(l, last):
        srcs = [ffn2_w_gate, ffn2_w_up, ffn2_w_down]
        jobs = [_cast_job(w, l, xgrid) for w in srcs]
        if not last:
            srcs += [w_in_t, pool_rows, w_out, xattn_wq, xattn_wkv, xattn_wo]
            jobs += [_cast_job_t(w_in_t, l + 1, xgrid, n_main), _cast_job(pool_rows, l + 1, xgrid),
                     _cast_job(w_out, l + 1, xgrid, panels=True),
                     _cast_job(xattn_wq, l + 1, xgrid, panels=True),
                     _cast_job(xattn_wkv, l + 1, xgrid),
                     _cast_job(xattn_wo, l + 1, xgrid, panels=True)]
        return jobs, srcs

    fw = [bf(ffn1_w_gate[0]), bf(ffn1_w_up[0]), bf(ffn1_w_down[0])]
    mw = [None, bf(pool_rows[0]), _to_panels(bf(w_out[0])), _to_panels(bf(xattn_wq[0])),
          bf(xattn_wkv[0]), _to_panels(bf(xattn_wo[0]))]
    h = x.reshape(B * S, D)
    mem2 = mem.reshape(B * M, D)
    for l in range(depth):
        last = l == depth - 1
        h, _ = _ffn(h, row(ffn1_norm), fw[0], fw[1], fw[2], fg, l, [], [], final=False)
        if l == 0:
            n_steps = n_main // LANES
            mw[0] = _copy_call(_cast_job_t(w_in_t, 0, (n_steps,), n_main), w_in_t, n_steps)
        w_main, poolw, wout, wq, wkv, wo = mw
        h = _mix(h, row(mix_norm), w_main, w_gate, gbias, qk_conv, row(head_norm),
                 poolw.reshape(pool_w.shape[1:]), row(pool_scale), wout, l, B=B, S=S)
        kv = _kv_proj(mem2, row(mem_norm), wkv, l)
        jobs, srcs = xattn_jobs(l, last)
        h, cw = _xattn(h, row(xattn_norm), wq, kv, wo, l, jobs, srcs, S=S, M=M)
        fw, mw = cw[:3], cw[3:]
        jobs, srcs = ([], []) if last else ffn_jobs((ffn1_w_gate, ffn1_w_up, ffn1_w_down), l + 1)
        h, cw = _ffn(h, row(ffn2_norm), fw[0], fw[1], fw[2], fg, l, jobs, srcs, final=last)
        fw = cw
    return h.reshape(B, S, D)
```

```python
import functools

import jax
import jax.numpy as jnp
from jax import lax
from jax.experimental import pallas as pl
from jax.experimental.pallas import tpu as pltpu

F32 = jnp.float32
BF16 = jnp.bfloat16

EPS = 1e-6
MLSTM_HEADS = 4
POOL_WINDOWS = (2, 4, 8, 16)
CONV_W = 4
XATTN_HEADS = 4
LANES = 128
BF16_ROWS = 16
PANEL = 512
POOL_HALO = 32
CONV_HALO = 8
MIB = 1024 * 1024

FFN_TM, FFN_TF, FFN_VMEM = 1024, 512, 60 * MIB
MIX_TILE, MIX_VMEM = 256, 56 * MIB
XATTN_TM, XATTN_VMEM = 512, 58 * MIB
KV_TM, KV_TN, KV_VMEM = 512, 1024, 32 * MIB


def _tile(n, pref):
    return pref if n % pref == 0 else n


def _rms(x, g):
    y = x * lax.rsqrt(jnp.mean(x * x, axis=-1, keepdims=True) + EPS)
    return y * g


def _silu(x):
    h = 0.5 * x
    return h * jnp.tanh(h) + h


def _log_sigmoid(x):
    return jnp.minimum(x, 0.0) - jnp.log1p(jnp.exp(-jnp.abs(x)))


def _params(vmem, *sem):
    return pltpu.CompilerParams(dimension_semantics=sem, vmem_limit_bytes=vmem)


_RESIDENT = dict(pipeline_mode=pl.Buffered(1))


def _panel_cols(ref, lo, n):
    p, off = divmod(lo, PANEL)
    assert off + n <= PANEL
    return ref[p, :, off:off + n]


def _to_panels(w):
    K, N = w.shape
    return w.reshape(K, N // PANEL, PANEL).transpose(1, 0, 2)


def _cast_job(w, layer, grid, panels=False):
    _, R, C = w.shape
    ni = grid[0]
    rep = 1
    while (R * rep) % (ni * BF16_ROWS) and rep < ni:
        rep *= 2
    nrb = ni // rep
    assert ni % rep == 0 and R % nrb == 0
    ncol = 1
    if len(grid) == 2:
        assert not panels
        ncol = max([n for n in range(2, grid[1] + 1) if C % (n * LANES) == 0] + [1])
        idx = lambda i, j: (i // rep, jnp.minimum(j, ncol - 1))
    else:
        idx = lambda i: (i // rep, 0)
    src = pl.BlockSpec((None, R // nrb, C // ncol), lambda *ij: (layer,) + idx(*ij))
    if panels:
        assert C % PANEL == 0
        dst = pl.BlockSpec((C // PANEL, R // nrb, PANEL), lambda i: (0, i // rep, 0))
        return src, dst, jax.ShapeDtypeStruct((C // PANEL, R, PANEL), BF16), "panels"
    dst = pl.BlockSpec((R // nrb, C // ncol), idx)
    return src, dst, jax.ShapeDtypeStruct((R, C), BF16), "plain"


def _cast_job_t(wt, layer, grid, ncols):
    R = wt.shape[-1]
    assert ncols % PANEL == 0
    per = pl.cdiv(ncols // LANES, grid[0])
    lpp = PANEL // LANES
    per = min(d for d in (1, 2, 4) if d >= per) if per <= lpp else lpp * pl.cdiv(per, lpp)
    assert lpp == 4
    cpb = per * LANES
    last = pl.cdiv(ncols, cpb) - 1
    blk = lambda *ij: jnp.minimum(ij[0], last)
    src = pl.BlockSpec((None, cpb, R), lambda *ij: (layer, blk(*ij), 0))
    if cpb <= PANEL:
        bpp = PANEL // cpb
        dst = pl.BlockSpec((None, R, cpb), lambda *ij: (blk(*ij) // bpp, 0, blk(*ij) % bpp))
    else:
        dst = pl.BlockSpec((cpb // PANEL, R, PANEL), lambda *ij: (blk(*ij), 0, 0))
    return src, dst, jax.ShapeDtypeStruct((ncols // PANEL, R, PANEL), BF16), "transposed"


def _copy_casts(src_refs, dst_refs, kinds):
    for src_ref, dst_ref, kind in zip(src_refs, dst_refs, kinds):
        v = src_ref[...]
        if kind == "transposed":
            v = v.T
        if dst_ref.ndim == 2:
            dst_ref[...] = v.astype(BF16)
        else:
            for p in range(dst_ref.shape[0]):
                dst_ref[p] = v[:, p * PANEL:(p + 1) * PANEL].astype(BF16)


def _copy_call(job, src, n_steps):
    def body(src_ref, dst_ref):
        _copy_casts([src_ref], [dst_ref], [job[3]])

    return pl.pallas_call(
        body, grid=(n_steps,), in_specs=[job[0]], out_specs=job[1], out_shape=job[2],
        compiler_params=_params(KV_VMEM, "arbitrary"), name="weight_copy")(src)


def _gate_rows(w_in_t, n_main):
    L, _, D = w_in_t.shape
    ng = 2 * MLSTM_HEADS
    assert n_main % ng == 0

    def body(src_ref, dst_ref):
        dst_ref[...] = src_ref[...]

    return pl.pallas_call(
        body, grid=(L,),
        in_specs=[pl.BlockSpec((None, ng, D), lambda l: (l, n_main // ng, 0))],
        out_specs=pl.BlockSpec((None, ng, D), lambda l: (l, 0, 0)),
        out_shape=jax.ShapeDtypeStruct((L, ng, D), F32),
        compiler_params=_params(KV_VMEM, "arbitrary"), name="gate_rows")(w_in_t)


def _ffn_kernel(*refs, final, kinds):
    nc = len(kinds)
    h_ref, g_ref, wg_ref, wu_ref, wd_ref, fg_ref = refs[:6]
    src_refs = refs[6:6 + nc]
    o_ref = refs[6 + nc]
    dst_refs = refs[7 + nc:7 + 2 * nc]
    xn_ref = refs[7 + 2 * nc]
    j = pl.program_id(1)

    @pl.when(j == 0)
    def _():
        x = h_ref[...]
        xn_ref[...] = _rms(x, g_ref[...]).astype(BF16)
        o_ref[...] = x

    xn = xn_ref[...]
    a = jnp.dot(xn, wg_ref[...], preferred_element_type=F32)
    b = jnp.dot(xn, wu_ref[...], preferred_element_type=F32)
    mid = (_silu(a) * b).astype(BF16)
    o_ref[...] += 0.5 * jnp.dot(mid, wd_ref[...], preferred_element_type=F32)

    _copy_casts(src_refs, dst_refs, kinds)

    if final:
        @pl.when(j == pl.num_programs(1) - 1)
        def _():
            o_ref[...] = _rms(o_ref[...], fg_ref[...])


def _ffn_grid(T, F):
    tm, tf = _tile(T, FFN_TM), _tile(F, FFN_TF)
    return T // tm, F // tf


def _ffn(h, g, wg, wu, wd, fg, layer, jobs, srcs, *, final):
    T, D = h.shape
    F = wg.shape[-1]
    ni, nj = _ffn_grid(T, F)
    tm, tf = T // ni, F // nj
    outs = pl.pallas_call(
        functools.partial(_ffn_kernel, final=final, kinds=tuple(jb[3] for jb in jobs)),
        grid=(ni, nj),
        in_specs=[
            pl.BlockSpec((tm, D), lambda i, j: (i, 0)),
            pl.BlockSpec((None, 1, D), lambda i, j: (layer, 0, 0)),
            pl.BlockSpec((D, tf), lambda i, j: (0, j)),
            pl.BlockSpec((D, tf), lambda i, j: (0, j)),
            pl.BlockSpec((tf, D), lambda i, j: (j, 0)),
            pl.BlockSpec((1, D), lambda i, j: (0, 0)),
        ] + [jb[0] for jb in jobs],
        out_specs=[pl.BlockSpec((tm, D), lambda i, j: (i, 0))] + [jb[1] for jb in jobs],
        out_shape=[jax.ShapeDtypeStruct((T, D), F32)] + [jb[2] for jb in jobs],
        scratch_shapes=[pltpu.VMEM((tm, D), BF16)],
        compiler_params=_params(FFN_VMEM, "arbitrary", "arbitrary"),
        name="ffn_final" if final else "ffn",
    )(h, g, wg, wu, wd, fg, *srcs)
    return outs[0], list(outs[1:])


def _scan_rows(x, op, fill, row):
    n = x.shape[0]
    sh = 1
    while sh < n:
        x = op(x, jnp.where(row >= sh, pltpu.roll(x, sh, axis=0), fill))
        sh *= 2
    return x


def _mix_kernel(h_ref, g_ref, win_ref, wgate_ref, gbias_ref, conv_ref, hn_ref, poolw_ref,
                pscale_ref, wout_ref,
                o_ref,
                pext_ref, s2_ref, s4_ref, cext_ref, c_ref, n_ref, m_ref,
                xn_ref, qs_ref, ks_ref, y_ref,
                *, ts, W):
    H = MLSTM_HEADS
    dh = W // H
    gw = W // len(POOL_WINDOWS)
    pw = min(PANEL, W)
    PH, CH = POOL_HALO, CONV_HALO
    s = pl.program_id(1)

    @pl.when(s == 0)
    def _():
        pext_ref[0:PH, :] = jnp.zeros((PH, W), F32)
        cext_ref[0:CH, :] = jnp.zeros((CH, 2 * W), F32)
        c_ref[...] = jnp.zeros_like(c_ref)
        n_ref[...] = jnp.zeros_like(n_ref)
        m_ref[...] = jnp.zeros_like(m_ref)

    x = h_ref[...]
    xn_ref[...] = _rms(x, g_ref[...]).astype(BF16)
    row = lax.broadcasted_iota(jnp.int32, (ts, 1), 0)

    def proj(lo, n):
        return jnp.dot(xn_ref[...], _panel_cols(win_ref, lo, n), preferred_element_type=F32)

    for c0 in range(0, W, pw):
        pext_ref[PH:PH + ts, c0:c0 + pw] = proj(c0, pw)
    tpos = row + s * ts + 1
    n_ext = PH + ts
    for g, win in enumerate(POOL_WINDOWS):
        cs = slice(g * gw, (g + 1) * gw)
        pg = pext_ref[PH:n_ext, cs]
        if win == 2:
            acc = pg + pext_ref[PH - 1:n_ext - 1, cs]
        else:
            s2_ref[8:n_ext, :] = pext_ref[8:n_ext, cs] + pext_ref[7:n_ext - 1, cs]
            if win == 4:
                acc = s2_ref[PH:n_ext, :] + s2_ref[PH - 2:n_ext - 2, :]
            else:
                s4_ref[16:n_ext, :] = s2_ref[16:n_ext, :] + s2_ref[14:n_ext - 2, :]
                if win == 8:
                    acc = s4_ref[PH:n_ext, :] + s4_ref[PH - 4:n_ext - 4, :]
                else:
                    s8 = s4_ref[24:n_ext, :] + s4_ref[20:n_ext - 4, :]
                    acc = s8[8:] + s8[:-8]
        inv_cnt = 1.0 / jnp.minimum(tpos, win).astype(F32)
        d = acc * inv_cnt - pg
        yg = jnp.dot(d.astype(BF16), poolw_ref[g], preferred_element_type=F32)
        y_ref[:, cs] = (yg * pscale_ref[:, cs]).astype(BF16)
    pext_ref[0:PH, :] = pext_ref[ts:ts + PH, :]

    nblk = W // dh
    for c0 in range(0, 2 * W, pw):
        cext_ref[CH:CH + ts, c0:c0 + pw] = proj(W + c0, pw)
    for blk in range(2 * nblk):
        hs = slice(blk * dh, (blk + 1) * dh)
        lo = CH - (CONV_W - 1)
        y = cext_ref[lo:lo + ts, hs] * conv_ref[0:1, hs]
        for j in range(1, CONV_W):
            y = y + cext_ref[lo + j:lo + j + ts, hs] * conv_ref[j:j + 1, hs]
        y = _silu(y)
        cext_ref[0:CH, hs] = cext_ref[ts:ts + CH, hs]
        if blk < nblk:
            qs_ref[:, hs] = (y * (dh ** -0.5)).astype(BF16)
        else:
            ks_ref[:, (blk - nblk) * dh:(blk - nblk + 1) * dh] = y

    gates = jnp.dot(xn_ref[...], wgate_ref[...], preferred_element_type=F32) + gbias_ref[...]
    li = gates[:, :LANES]
    lf = _log_sigmoid(gates[:, LANES:])
    b = _scan_rows(lf, jnp.add, 0.0, row)
    a = li - b
    m_prev = m_ref[...]
    u = jnp.maximum(m_prev, _scan_rows(a, jnp.maximum, -jnp.inf, row))
    u_l = u[ts - 1:ts, :]
    m_ref[...] = b[ts - 1:ts, :] + u_l
    iw = jnp.exp(m_prev - u)
    emt = jnp.exp(-b - u)
    ws = jnp.exp(a - u_l)
    decay = jnp.exp(m_prev - u_l)
    a_row = a.T
    tri = (lax.broadcasted_iota(jnp.int32, (ts, ts), 0)
           >= lax.broadcasted_iota(jnp.int32, (ts, ts), 1))

    for hd in range(H):
        cs = slice(hd * dh, (hd + 1) * dh)
        q_c = qs_ref[:, cs]
        k_c = ks_ref[:, cs]
        v_c = proj(3 * W + hd * dh, dh).astype(BF16)
        og = proj(4 * W + hd * dh, dh)
        sc = lax.dot_general(q_c, k_c.astype(BF16), (((1,), (1,)), ((), ())),
                             preferred_element_type=F32)
        e = jnp.where(tri, jnp.exp(a_row[hd:hd + 1, :] - u[:, hd:hd + 1]), 0.0)
        wm = e * sc
        cst = c_ref[hd]
        nst = n_ref[hd]
        iw_h = iw[:, hd:hd + 1]
        num = (iw_h * jnp.dot(q_c, cst.astype(BF16), preferred_element_type=F32)
               + jnp.dot(wm.astype(BF16), v_c, preferred_element_type=F32))
        den = (iw_h * jnp.sum(q_c.astype(F32) * nst, axis=-1, keepdims=True)
               + jnp.sum(wm, axis=-1, keepdims=True))
        hh = num * (1.0 / jnp.maximum(jnp.abs(den), emt[:, hd:hd + 1]))
        hh = hh * lax.rsqrt(jnp.mean(hh * hh, axis=-1, keepdims=True) + EPS)
        yv = hh * hn_ref[:, cs] * jax.nn.sigmoid(og)
        y_ref[:, W + hd * dh:W + (hd + 1) * dh] = yv.astype(BF16)
        kw = k_c * ws[:, hd:hd + 1]
        dec_h = decay[:, hd:hd + 1]
        c_ref[hd] = dec_h * cst + lax.dot_general(
            kw.astype(BF16), v_c, (((0,), (0,)), ((), ())), preferred_element_type=F32)
        n_ref[hd] = dec_h * nst + jnp.sum(kw, axis=0, keepdims=True)

    for p in range(wout_ref.shape[0]):
        cs = slice(p * PANEL, (p + 1) * PANEL)
        o_ref[:, cs] = x[:, cs] + jnp.dot(y_ref[...], wout_ref[p], preferred_element_type=F32)


def _mix(h, g, w_main, w_gate, gbias, conv_w, head_norm, pool_w, pool_scale, w_out, layer, *, B, S):
    T, D = h.shape
    W = D // 2
    H = MLSTM_HEADS
    dh = W // H
    ng = len(POOL_WINDOWS)
    gw = W // ng
    ts = _tile(S, MIX_TILE)
    nst = S // ts
    return pl.pallas_call(
        functools.partial(_mix_kernel, ts=ts, W=W),
        grid=(B, nst),
        in_specs=[
            pl.BlockSpec((ts, D), lambda b, s: (b * nst + s, 0)),
            pl.BlockSpec((None, 1, D), lambda b, s: (layer, 0, 0)),
            pl.BlockSpec(w_main.shape, lambda b, s: (0, 0, 0), **_RESIDENT),
            pl.BlockSpec((None, D, 2 * LANES), lambda b, s: (layer, 0, 0), **_RESIDENT),
            pl.BlockSpec((None, 1, 2 * LANES), lambda b, s: (layer, 0, 0)),
            pl.BlockSpec((None, CONV_W, 2 * W), lambda b, s: (layer, 0, 0)),
            pl.BlockSpec((None, 1, W), lambda b, s: (layer, 0, 0)),
            pl.BlockSpec((ng, gw, gw), lambda b, s: (0, 0, 0)),
            pl.BlockSpec((None, 1, W), lambda b, s: (layer, 0, 0)),
            pl.BlockSpec(w_out.shape, lambda b, s: (0, 0, 0), **_RESIDENT),
        ],
        out_specs=pl.BlockSpec((ts, D), lambda b, s: (b * nst + s, 0)),
        out_shape=jax.ShapeDtypeStruct((T, D), F32),
        scratch_shapes=[
            pltpu.VMEM((POOL_HALO + ts, W), F32),
            pltpu.VMEM((POOL_HALO + ts, gw), F32),
            pltpu.VMEM((POOL_HALO + ts, gw), F32),
            pltpu.VMEM((CONV_HALO + ts, 2 * W), F32),
            pltpu.VMEM((H, dh, dh), F32),
            pltpu.VMEM((H, 1, dh), F32),
            pltpu.VMEM((1, LANES), F32),
            pltpu.VMEM((ts, D), BF16),
            pltpu.VMEM((ts, W), BF16),
            pltpu.VMEM((ts, W), F32),
            pltpu.VMEM((ts, D), BF16),
        ],
        compiler_params=_params(MIX_VMEM, "arbitrary", "arbitrary"),
        name="mix",
    )(h, g, w_main, w_gate, gbias, conv_w, head_norm, pool_w, pool_scale, w_out)


def _kv_kernel(m_ref, g_ref, w_ref, o_ref, xn_ref):
    j = pl.program_id(1)

    @pl.when(j == 0)
    def _():
        xn_ref[...] = _rms(m_ref[...], g_ref[...]).astype(BF16)

    o_ref[...] = jnp.dot(xn_ref[...], w_ref[...], preferred_element_type=F32).astype(BF16)


def _kv_proj(mem, g, wkv, layer):
    T, D = mem.shape
    N = wkv.shape[-1]
    tm = _tile(T, KV_TM)
    tn = _tile(N, KV_TN)
    return pl.pallas_call(
        _kv_kernel,
        grid=(T // tm, N // tn),
        in_specs=[
            pl.BlockSpec((tm, D), lambda i, j: (i, 0)),
            pl.BlockSpec((None, 1, D), lambda i, j: (layer, 0, 0)),
            pl.BlockSpec((D, tn), lambda i, j: (0, j)),
        ],
        out_specs=pl.BlockSpec((tm, tn), lambda i, j: (i, j)),
        out_shape=jax.ShapeDtypeStruct((T, N), BF16),
        scratch_shapes=[pltpu.VMEM((tm, D), BF16)],
        compiler_params=_params(KV_VMEM, "parallel", "arbitrary"),
        name="xattn_kv",
    )(mem, g, wkv)


def _xattn_kernel(*refs, D, kinds):
    nc = len(kinds)
    h_ref, g_ref, wq_ref, kv_ref, wo_ref = refs[:5]
    o_ref = refs[5 + nc]
    xn_ref, ao_ref = refs[6 + 2 * nc:]
    H = XATTN_HEADS
    dh = D // H
    x = h_ref[...]
    xn_ref[...] = _rms(x, g_ref[...]).astype(BF16)
    for hd in range(H):
        cs = slice(hd * dh, (hd + 1) * dh)
        q = jnp.dot(xn_ref[...], _panel_cols(wq_ref, hd * dh, dh), preferred_element_type=F32)
        s = lax.dot_general(q.astype(BF16), kv_ref[:, cs], (((1,), (1,)), ((), ())),
                            preferred_element_type=F32) * (dh ** -0.5)
        e = jnp.exp(s - jnp.max(s, axis=-1, keepdims=True))
        pr = e * (1.0 / jnp.sum(e, axis=-1, keepdims=True))
        ao_ref[:, cs] = jnp.dot(pr.astype(BF16), kv_ref[:, D + hd * dh:D + (hd + 1) * dh],
                                preferred_element_type=F32).astype(BF16)
    for p in range(wo_ref.shape[0]):
        cs = slice(p * PANEL, (p + 1) * PANEL)
        o_ref[:, cs] = x[:, cs] + jnp.dot(ao_ref[...], wo_ref[p], preferred_element_type=F32)
    _copy_casts(refs[5:5 + nc], refs[6 + nc:6 + 2 * nc], kinds)


def _xattn(h, g, wq, kv, wo, layer, jobs, srcs, *, S, M):
    T, D = h.shape
    tm = _tile(S, XATTN_TM)
    nsb = S // tm
    outs = pl.pallas_call(
        functools.partial(_xattn_kernel, D=D, kinds=tuple(jb[3] for jb in jobs)),
        grid=(T // tm,),
        in_specs=[
            pl.BlockSpec((tm, D), lambda i: (i, 0)),
            pl.BlockSpec((None, 1, D), lambda i: (layer, 0, 0)),
            pl.BlockSpec(wq.shape, lambda i: (0, 0, 0), **_RESIDENT),
            pl.BlockSpec((M, 2 * D), lambda i: (i // nsb, 0)),
            pl.BlockSpec(wo.shape, lambda i: (0, 0, 0), **_RESIDENT),
        ] + [jb[0] for jb in jobs],
        out_specs=[pl.BlockSpec((tm, D), lambda i: (i, 0))] + [jb[1] for jb in jobs],
        out_shape=[jax.ShapeDtypeStruct((T, D), F32)] + [jb[2] for jb in jobs],
        scratch_shapes=[pltpu.VMEM((tm, D), BF16), pltpu.VMEM((tm, D), BF16)],
        compiler_params=_params(XATTN_VMEM, "arbitrary"),
        name="xattn",
    )(h, g, wq, kv, wo, *srcs)
    return outs[0], list(outs[1:])


def kernel(x, mem, ffn1_norm, ffn1_w_gate, ffn1_w_up, ffn1_w_down, mix_norm, w_in, gate_bias, qk_conv, head_norm, pool_w, pool_scale, w_out, xattn_norm, mem_norm, xattn_wq, xattn_wkv, xattn_wo, ffn2_norm, ffn2_w_gate, ffn2_w_up, ffn2_w_down, final_norm):
    B, S, D = x.shape
    M = mem.shape[1]
    depth = ffn1_norm.shape[0]
    W = D // 2
    H = MLSTM_HEADS
    n_main = 5 * W

    bf = lambda w: w.astype(BF16)
    row = lambda v: v.reshape(v.shape[0], 1, v.shape[-1])

    w_in_t = jnp.swapaxes(w_in, 1, 2)
    lane_pad = lambda v: jnp.pad(v, [(0, 0)] * (v.ndim - 1) + [(0, LANES - H)])
    gate_cols = jnp.swapaxes(_gate_rows(w_in_t, n_main), 1, 2)
    w_gate = bf(jnp.concatenate([lane_pad(gate_cols[..., :H]), lane_pad(gate_cols[..., H:])], axis=-1))
    gbias = jnp.concatenate([lane_pad(gate_bias[:, :H]), lane_pad(gate_bias[:, H:])],
                            axis=-1).reshape(depth, 1, 2 * LANES)
    fg = final_norm.reshape(1, D)
    pool_rows = pool_w.reshape(depth, -1, pool_w.shape[-1])

    fgrid = _ffn_grid(B * S, ffn1_w_gate.shape[-1])

    def ffn_jobs(ws, l):
        return [_cast_job(w, l, fgrid) for w in ws], list(ws)

    xgrid = (B * S // _tile(S, XATTN_TM),)

    def xattn_jobs(l, last):
        srcs = [ffn2_w_gate, ffn2_w_up, ffn2_w_down]
        jobs = [_cast_job(w, l, xgrid) for w in srcs]
        if not last:
            srcs += [w_in_t, pool_rows, w_out, xattn_wq, xattn_wkv, xattn_wo]
            jobs += [_cast_job_t(w_in_t, l + 1, xgrid, n_main), _cast_job(pool_rows, l + 1, xgrid),
                     _cast_job(w_out, l + 1, xgrid, panels=True),
                     _cast_job(xattn_wq, l + 1, xgrid, panels=True),
                     _cast_job(xattn_wkv, l + 1, xgrid),
                     _cast_job(xattn_wo, l + 1, xgrid, panels=True)]
        return jobs, srcs

    fw = [bf(ffn1_w_gate[0]), bf(ffn1_w_up[0]), bf(ffn1_w_down[0])]
    mw = [None, bf(pool_rows[0]), _to_panels(bf(w_out[0])), _to_panels(bf(xattn_wq[0])),
          bf(xattn_wkv[0]), _to_panels(bf(xattn_wo[0]))]
    h = x.reshape(B * S, D)
    mem2 = mem.reshape(B * M, D)
    for l in range(depth):
        last = l == depth - 1
        h, _ = _ffn(h, row(ffn1_norm), fw[0], fw[1], fw[2], fg, l, [], [], final=False)
        if l == 0:
            n_steps = n_main // LANES
            mw[0] = _copy_call(_cast_job_t(w_in_t, 0, (n_steps,), n_main), w_in_t, n_steps)
        w_main, poolw, wout, wq, wkv, wo = mw
        h = _mix(h, row(mix_norm), w_main, w_gate, gbias, qk_conv, row(head_norm),
                 poolw.reshape(pool_w.shape[1:]), row(pool_scale), wout, l, B=B, S=S)
        kv = _kv_proj(mem2, row(mem_norm), wkv, l)
        jobs, srcs = xattn_jobs(l, last)
        h, cw = _xattn(h, row(xattn_norm), wq, kv, wo, l, jobs, srcs, S=S, M=M)
        fw, mw = cw[:3], cw[3:]
        jobs, srcs = ([], []) if last else ffn_jobs((ffn1_w_gate, ffn1_w_up, ffn1_w_down), l + 1)
        h, cw = _ffn(h, row(ffn2_norm), fw[0], fw[1], fw[2], fg, l, jobs, srcs, final=last)
        fw = cw
    return h.reshape(B, S, D)
```

```python
import functools

import jax
import jax.numpy as jnp
from jax import lax
from jax.experimental import pallas as pl
from jax.experimental.pallas import tpu as pltpu

F32 = jnp.float32
BF16 = jnp.bfloat16

EPS = 1e-6
MLSTM_HEADS = 4
POOL_WINDOWS = (2, 4, 8, 16)
CONV_W = 4
XATTN_HEADS = 4
LANES = 128
BF16_ROWS = 16
PANEL = 512
POOL_HALO = 32
CONV_HALO = 8
MIB = 1024 * 1024

FFN_TM, FFN_TF, FFN_VMEM = 1024, 512, 60 * MIB
MIX_TILE, MIX_VMEM = 256, 56 * MIB
XATTN_TM, XATTN_VMEM = 512, 58 * MIB
KV_TM, KV_TN, KV_VMEM = 512, 1024, 32 * MIB


def _tile(n, pref):
    return pref if n % pref == 0 else n


def _rms(x, g):
    y = x * lax.rsqrt(jnp.mean(x * x, axis=-1, keepdims=True) + EPS)
    return y * g


def _silu(x):
    h = 0.5 * x
    return h * jnp.tanh(h) + h


def _log_sigmoid(x):
    return jnp.minimum(x, 0.0) - jnp.log1p(jnp.exp(-jnp.abs(x)))


def _params(vmem, *sem):
    return pltpu.CompilerParams(dimension_semantics=sem, vmem_limit_bytes=vmem)


_RESIDENT = dict(pipeline_mode=pl.Buffered(1))


def _panel_cols(ref, lo, n):
    p, off = divmod(lo, PANEL)
    assert off + n <= PANEL
    return ref[p, :, off:off + n]


def _to_panels(w):
    K, N = w.shape
    return w.reshape(K, N // PANEL, PANEL).transpose(1, 0, 2)


def _cast_job(w, layer, grid, panels=False):
    _, R, C = w.shape
    ni = grid[0]
    rep = 1
    while (R * rep) % (ni * BF16_ROWS) and rep < ni:
        rep *= 2
    nrb = ni // rep
    assert ni % rep == 0 and R % nrb == 0
    ncol = 1
    if len(grid) == 2:
        assert not panels
        ncol = max([n for n in range(2, grid[1] + 1) if C % (n * LANES) == 0] + [1])
        idx = lambda i, j: (i // rep, jnp.minimum(j, ncol - 1))
    else:
        idx = lambda i: (i // rep, 0)
    src = pl.BlockSpec((None, R // nrb, C // ncol), lambda *ij: (layer,) + idx(*ij))
    if panels:
        assert C % PANEL == 0
        dst = pl.BlockSpec((C // PANEL, R // nrb, PANEL), lambda i: (0, i // rep, 0))
        return src, dst, jax.ShapeDtypeStruct((C // PANEL, R, PANEL), BF16), "panels"
    dst = pl.BlockSpec((R // nrb, C // ncol), idx)
    return src, dst, jax.ShapeDtypeStruct((R, C), BF16), "plain"


def _cast_job_t(wt, layer, grid, ncols):
    R = wt.shape[-1]
    assert ncols % PANEL == 0
    per = pl.cdiv(ncols // LANES, grid[0])
    lpp = PANEL // LANES
    if per <= lpp:
        per = min(d for d in range(per, lpp + 1) if lpp % d == 0)
    else:
        per = lpp * pl.cdiv(per, lpp)
    cpb = per * LANES
    last = pl.cdiv(ncols, cpb) - 1
    blk = lambda *ij: jnp.minimum(ij[0], last)
    src = pl.BlockSpec((None, cpb, R), lambda *ij: (layer, blk(*ij), 0))
    if cpb <= PANEL:
        bpp = PANEL // cpb
        dst = pl.BlockSpec((None, R, cpb), lambda *ij: (blk(*ij) // bpp, 0, blk(*ij) % bpp))
    else:
        dst = pl.BlockSpec((cpb // PANEL, R, PANEL), lambda *ij: (blk(*ij), 0, 0))
    return src, dst, jax.ShapeDtypeStruct((ncols // PANEL, R, PANEL), BF16), "transposed"


def _copy_casts(src_refs, dst_refs, kinds):
    for src_ref, dst_ref, kind in zip(src_refs, dst_refs, kinds):
        v = src_ref[...]
        if kind == "transposed":
            v = v.T
        if dst_ref.ndim == 2:
            dst_ref[...] = v.astype(BF16)
        else:
            for p in range(dst_ref.shape[0]):
                dst_ref[p] = v[:, p * PANEL:(p + 1) * PANEL].astype(BF16)


def _copy_call(job, src, n_steps):
    def body(src_ref, dst_ref):
        _copy_casts([src_ref], [dst_ref], [job[3]])

    return pl.pallas_call(
        body, grid=(n_steps,), in_specs=[job[0]], out_specs=job[1], out_shape=job[2],
        compiler_params=_params(KV_VMEM, "arbitrary"), name="weight_copy")(src)


def _gate_rows(w_in_t, n_main):
    L, _, D = w_in_t.shape
    ng = 2 * MLSTM_HEADS
    assert n_main % ng == 0

    def body(src_ref, dst_ref):
        dst_ref[...] = src_ref[...]

    return pl.pallas_call(
        body, grid=(L,),
        in_specs=[pl.BlockSpec((None, ng, D), lambda l: (l, n_main // ng, 0))],
        out_specs=pl.BlockSpec((None, ng, D), lambda l: (l, 0, 0)),
        out_shape=jax.ShapeDtypeStruct((L, ng, D), F32),
        compiler_params=_params(KV_VMEM, "arbitrary"), name="gate_rows")(w_in_t)


def _ffn_kernel(*refs, final, kinds):
    nc = len(kinds)
    h_ref, g_ref, wg_ref, wu_ref, wd_ref, fg_ref = refs[:6]
    src_refs = refs[6:6 + nc]
    o_ref = refs[6 + nc]
    dst_refs = refs[7 + nc:7 + 2 * nc]
    xn_ref = refs[7 + 2 * nc]
    j = pl.program_id(1)

    @pl.when(j == 0)
    def _():
        x = h_ref[...]
        xn_ref[...] = _rms(x, g_ref[...]).astype(BF16)
        o_ref[...] = x

    xn = xn_ref[...]
    a = jnp.dot(xn, wg_ref[...], preferred_element_type=F32)
    b = jnp.dot(xn, wu_ref[...], preferred_element_type=F32)
    mid = (_silu(a) * b).astype(BF16)
    o_ref[...] += 0.5 * jnp.dot(mid, wd_ref[...], preferred_element_type=F32)

    _copy_casts(src_refs, dst_refs, kinds)

    if final:
        @pl.when(j == pl.num_programs(1) - 1)
        def _():
            o_ref[...] = _rms(o_ref[...], fg_ref[...])


def _ffn_grid(T, F):
    tm, tf = _tile(T, FFN_TM), _tile(F, FFN_TF)
    return T // tm, F // tf


def _ffn(h, g, wg, wu, wd, fg, layer, jobs, srcs, *, final):
    T, D = h.shape
    F = wg.shape[-1]
    ni, nj = _ffn_grid(T, F)
    tm, tf = T // ni, F // nj
    outs = pl.pallas_call(
        functools.partial(_ffn_kernel, final=final, kinds=tuple(jb[3] for jb in jobs)),
        grid=(ni, nj),
        in_specs=[
            pl.BlockSpec((tm, D), lambda i, j: (i, 0)),
            pl.BlockSpec((None, 1, D), lambda i, j: (layer, 0, 0)),
            pl.BlockSpec((D, tf), lambda i, j: (0, j)),
            pl.BlockSpec((D, tf), lambda i, j: (0, j)),
            pl.BlockSpec((tf, D), lambda i, j: (j, 0)),
            pl.BlockSpec((1, D), lambda i, j: (0, 0)),
        ] + [jb[0] for jb in jobs],
        out_specs=[pl.BlockSpec((tm, D), lambda i, j: (i, 0))] + [jb[1] for jb in jobs],
        out_shape=[jax.ShapeDtypeStruct((T, D), F32)] + [jb[2] for jb in jobs],
        scratch_shapes=[pltpu.VMEM((tm, D), BF16)],
        compiler_params=_params(FFN_VMEM, "arbitrary", "arbitrary"),
        name="ffn_final" if final else "ffn",
    )(h, g, wg, wu, wd, fg, *srcs)
    return outs[0], list(outs[1:])


def _scan_rows(x, op, fill, row):
    n = x.shape[0]
    sh = 1
    while sh < n:
        x = op(x, jnp.where(row >= sh, pltpu.roll(x, sh, axis=0), fill))
        sh *= 2
    return x


def _mix_kernel(h_ref, g_ref, win_ref, wgate_ref, gbias_ref, conv_ref, hn_ref, poolw_ref,
                pscale_ref, wout_ref,
                o_ref,
                pext_ref, s2_ref, s4_ref, cext_ref, c_ref, n_ref, m_ref,
                xn_ref, qs_ref, ks_ref, y_ref,
                *, ts, W):
    H = MLSTM_HEADS
    dh = W // H
    gw = W // len(POOL_WINDOWS)
    pw = min(PANEL, W)
    PH, CH = POOL_HALO, CONV_HALO
    s = pl.program_id(1)

    @pl.when(s == 0)
    def _():
        pext_ref[0:PH, :] = jnp.zeros((PH, W), F32)
        cext_ref[0:CH, :] = jnp.zeros((CH, 2 * W), F32)
        c_ref[...] = jnp.zeros_like(c_ref)
        n_ref[...] = jnp.zeros_like(n_ref)
        m_ref[...] = jnp.zeros_like(m_ref)

    x = h_ref[...]
    xn_ref[...] = _rms(x, g_ref[...]).astype(BF16)
    row = lax.broadcasted_iota(jnp.int32, (ts, 1), 0)

    def proj(lo, n):
        return jnp.dot(xn_ref[...], _panel_cols(win_ref, lo, n), preferred_element_type=F32)

    for c0 in range(0, W, pw):
        pext_ref[PH:PH + ts, c0:c0 + pw] = proj(c0, pw)
    tpos = row + s * ts + 1
    n_ext = PH + ts
    for g, win in enumerate(POOL_WINDOWS):
        cs = slice(g * gw, (g + 1) * gw)
        pg = pext_ref[PH:n_ext, cs]
        if win == 2:
            acc = pg + pext_ref[PH - 1:n_ext - 1, cs]
        else:
            s2_ref[8:n_ext, :] = pext_ref[8:n_ext, cs] + pext_ref[7:n_ext - 1, cs]
            if win == 4:
                acc = s2_ref[PH:n_ext, :] + s2_ref[PH - 2:n_ext - 2, :]
            else:
                s4_ref[16:n_ext, :] = s2_ref[16:n_ext, :] + s2_ref[14:n_ext - 2, :]
                if win == 8:
                    acc = s4_ref[PH:n_ext, :] + s4_ref[PH - 4:n_ext - 4, :]
                else:
                    s8 = s4_ref[24:n_ext, :] + s4_ref[20:n_ext - 4, :]
                    acc = s8[8:] + s8[:-8]
        inv_cnt = 1.0 / jnp.minimum(tpos, win).astype(F32)
        d = acc * inv_cnt - pg
        yg = jnp.dot(d.astype(BF16), poolw_ref[g], preferred_element_type=F32)
        y_ref[:, cs] = (yg * pscale_ref[:, cs]).astype(BF16)
    pext_ref[0:PH, :] = pext_ref[ts:ts + PH, :]

    nblk = W // dh
    for c0 in range(0, 2 * W, pw):
        cext_ref[CH:CH + ts, c0:c0 + pw] = proj(W + c0, pw)
    for blk in range(2 * nblk):
        hs = slice(blk * dh, (blk + 1) * dh)
        lo = CH - (CONV_W - 1)
        y = cext_ref[lo:lo + ts, hs] * conv_ref[0:1, hs]
        for j in range(1, CONV_W):
            y = y + cext_ref[lo + j:lo + j + ts, hs] * conv_ref[j:j + 1, hs]
        y = _silu(y)
        cext_ref[0:CH, hs] = cext_ref[ts:ts + CH, hs]
        if blk < nblk:
            qs_ref[:, hs] = (y * (dh ** -0.5)).astype(BF16)
        else:
            ks_ref[:, (blk - nblk) * dh:(blk - nblk + 1) * dh] = y

    gates = jnp.dot(xn_ref[...], wgate_ref[...], preferred_element_type=F32) + gbias_ref[...]
    li = gates[:, :LANES]
    lf = _log_sigmoid(gates[:, LANES:])
    b = _scan_rows(lf, jnp.add, 0.0, row)
    a = li - b
    m_prev = m_ref[...]
    u = jnp.maximum(m_prev, _scan_rows(a, jnp.maximum, -jnp.inf, row))
    u_l = u[ts - 1:ts, :]
    m_ref[...] = b[ts - 1:ts, :] + u_l
    iw = jnp.exp(m_prev - u)
    emt = jnp.exp(-b - u)
    ws = jnp.exp(a - u_l)
    decay = jnp.exp(m_prev - u_l)
    a_row = a.T
    tri = (lax.broadcasted_iota(jnp.int32, (ts, ts), 0)
           >= lax.broadcasted_iota(jnp.int32, (ts, ts), 1))

    for hd in range(H):
        cs = slice(hd * dh, (hd + 1) * dh)
        q_c = qs_ref[:, cs]
        k_c = ks_ref[:, cs]
        v_c = proj(3 * W + hd * dh, dh).astype(BF16)
        og = proj(4 * W + hd * dh, dh)
        sc = lax.dot_general(q_c, k_c.astype(BF16), (((1,), (1,)), ((), ())),
                             preferred_element_type=F32)
        e = jnp.where(tri, jnp.exp(a_row[hd:hd + 1, :] - u[:, hd:hd + 1]), 0.0)
        wm = e * sc
        cst = c_ref[hd]
        nst = n_ref[hd]
        iw_h = iw[:, hd:hd + 1]
        num = (iw_h * jnp.dot(q_c, cst.astype(BF16), preferred_element_type=F32)
               + jnp.dot(wm.astype(BF16), v_c, preferred_element_type=F32))
        den = (iw_h * jnp.sum(q_c.astype(F32) * nst, axis=-1, keepdims=True)
               + jnp.sum(wm, axis=-1, keepdims=True))
        hh = num * (1.0 / jnp.maximum(jnp.abs(den), emt[:, hd:hd + 1]))
        hh = hh * lax.rsqrt(jnp.mean(hh * hh, axis=-1, keepdims=True) + EPS)
        yv = hh * hn_ref[:, cs] * jax.nn.sigmoid(og)
        y_ref[:, W + hd * dh:W + (hd + 1) * dh] = yv.astype(BF16)
        kw = k_c * ws[:, hd:hd + 1]
        dec_h = decay[:, hd:hd + 1]
        c_ref[hd] = dec_h * cst + lax.dot_general(
            kw.astype(BF16), v_c, (((0,), (0,)), ((), ())), preferred_element_type=F32)
        n_ref[hd] = dec_h * nst + jnp.sum(kw, axis=0, keepdims=True)

    for p in range(wout_ref.shape[0]):
        cs = slice(p * PANEL, (p + 1) * PANEL)
        o_ref[:, cs] = x[:, cs] + jnp.dot(y_ref[...], wout_ref[p], preferred_element_type=F32)


def _mix(h, g, w_main, w_gate, gbias, conv_w, head_norm, pool_w, pool_scale, w_out, layer, *, B, S):
    T, D = h.shape
    W = D // 2
    H = MLSTM_HEADS
    dh = W // H
    ng = len(POOL_WINDOWS)
    gw = W // ng
    ts = _tile(S, MIX_TILE)
    nst = S // ts
    return pl.pallas_call(
        functools.partial(_mix_kernel, ts=ts, W=W),
        grid=(B, nst),
        in_specs=[
            pl.BlockSpec((ts, D), lambda b, s: (b * nst + s, 0)),
            pl.BlockSpec((None, 1, D), lambda b, s: (layer, 0, 0)),
            pl.BlockSpec(w_main.shape, lambda b, s: (0, 0, 0), **_RESIDENT),
            pl.BlockSpec((None, D, 2 * LANES), lambda b, s: (layer, 0, 0), **_RESIDENT),
            pl.BlockSpec((None, 1, 2 * LANES), lambda b, s: (layer, 0, 0)),
            pl.BlockSpec((None, CONV_W, 2 * W), lambda b, s: (layer, 0, 0)),
            pl.BlockSpec((None, 1, W), lambda b, s: (layer, 0, 0)),
            pl.BlockSpec((ng, gw, gw), lambda b, s: (0, 0, 0)),
            pl.BlockSpec((None, 1, W), lambda b, s: (layer, 0, 0)),
            pl.BlockSpec(w_out.shape, lambda b, s: (0, 0, 0), **_RESIDENT),
        ],
        out_specs=pl.BlockSpec((ts, D), lambda b, s: (b * nst + s, 0)),
        out_shape=jax.ShapeDtypeStruct((T, D), F32),
        scratch_shapes=[
            pltpu.VMEM((POOL_HALO + ts, W), F32),
            pltpu.VMEM((POOL_HALO + ts, gw), F32),
            pltpu.VMEM((POOL_HALO + ts, gw), F32),
            pltpu.VMEM((CONV_HALO + ts, 2 * W), F32),
            pltpu.VMEM((H, dh, dh), F32),
            pltpu.VMEM((H, 1, dh), F32),
            pltpu.VMEM((1, LANES), F32),
            pltpu.VMEM((ts, D), BF16),
            pltpu.VMEM((ts, W), BF16),
            pltpu.VMEM((ts, W), F32),
            pltpu.VMEM((ts, D), BF16),
        ],
        compiler_params=_params(MIX_VMEM, "arbitrary", "arbitrary"),
        name="mix",
    )(h, g, w_main, w_gate, gbias, conv_w, head_norm, pool_w, pool_scale, w_out)


def _kv_kernel(m_ref, g_ref, w_ref, o_ref, xn_ref):
    j = pl.program_id(1)

    @pl.when(j == 0)
    def _():
        xn_ref[...] = _rms(m_ref[...], g_ref[...]).astype(BF16)

    o_ref[...] = jnp.dot(xn_ref[...], w_ref[...], preferred_element_type=F32).astype(BF16)


def _kv_proj(mem, g, wkv, layer):
    T, D = mem.shape
    N = wkv.shape[-1]
    tm = _tile(T, KV_TM)
    tn = _tile(N, KV_TN)
    return pl.pallas_call(
        _kv_kernel,
        grid=(T // tm, N // tn),
        in_specs=[
            pl.BlockSpec((tm, D), lambda i, j: (i, 0)),
            pl.BlockSpec((None, 1, D), lambda i, j: (layer, 0, 0)),
            pl.BlockSpec((D, tn), lambda i, j: (0, j)),
        ],
        out_specs=pl.BlockSpec((tm, tn), lambda i, j: (i, j)),
        out_shape=jax.ShapeDtypeStruct((T, N), BF16),
        scratch_shapes=[pltpu.VMEM((tm, D), BF16)],
        compiler_params=_params(KV_VMEM, "parallel", "arbitrary"),
        name="xattn_kv",
    )(mem, g, wkv)


def _xattn_kernel(*refs, D, kinds):
    nc = len(kinds)
    h_ref, g_ref, wq_ref, kv_ref, wo_ref = refs[:5]
    o_ref = refs[5 + nc]
    xn_ref, ao_ref = refs[6 + 2 * nc:]
    H = XATTN_HEADS
    dh = D // H
    x = h_ref[...]
    xn_ref[...] = _rms(x, g_ref[...]).astype(BF16)
    for hd in range(H):
        cs = slice(hd * dh, (hd + 1) * dh)
        q = jnp.dot(xn_ref[...], _panel_cols(wq_ref, hd * dh, dh), preferred_element_type=F32)
        s = lax.dot_general(q.astype(BF16), kv_ref[:, cs], (((1,), (1,)), ((), ())),
                            preferred_element_type=F32) * (dh ** -0.5)
        e = jnp.exp(s - jnp.max(s, axis=-1, keepdims=True))
        pr = e * (1.0 / jnp.sum(e, axis=-1, keepdims=True))
        ao_ref[:, cs] = jnp.dot(pr.astype(BF16), kv_ref[:, D + hd * dh:D + (hd + 1) * dh],
                                preferred_element_type=F32).astype(BF16)
    for p in range(wo_ref.shape[0]):
        cs = slice(p * PANEL, (p + 1) * PANEL)
        o_ref[:, cs] = x[:, cs] + jnp.dot(ao_ref[...], wo_ref[p], preferred_element_type=F32)
    _copy_casts(refs[5:5 + nc], refs[6 + nc:6 + 2 * nc], kinds)


def _xattn(h, g, wq, kv, wo, layer, jobs, srcs, *, S, M):
    T, D = h.shape
    tm = _tile(S, XATTN_TM)
    nsb = S // tm
    outs = pl.pallas_call(
        functools.partial(_xattn_kernel, D=D, kinds=tuple(jb[3] for jb in jobs)),
        grid=(T // tm,),
        in_specs=[
            pl.BlockSpec((tm, D), lambda i: (i, 0)),
            pl.BlockSpec((None, 1, D), lambda i: (layer, 0, 0)),
            pl.BlockSpec(wq.shape, lambda i: (0, 0, 0), **_RESIDENT),
            pl.BlockSpec((M, 2 * D), lambda i: (i // nsb, 0)),
            pl.BlockSpec(wo.shape, lambda i: (0, 0, 0), **_RESIDENT),
        ] + [jb[0] for jb in jobs],
        out_specs=[pl.BlockSpec((tm, D), lambda i: (i, 0))] + [jb[1] for jb in jobs],
        out_shape=[jax.ShapeDtypeStruct((T, D), F32)] + [jb[2] for jb in jobs],
        scratch_shapes=[pltpu.VMEM((tm, D), BF16), pltpu.VMEM((tm, D), BF16)],
        compiler_params=_params(XATTN_VMEM, "arbitrary"),
        name="xattn",
    )(h, g, wq, kv, wo, *srcs)
    return outs[0], list(outs[1:])


def kernel(x, mem, ffn1_norm, ffn1_w_gate, ffn1_w_up, ffn1_w_down, mix_norm, w_in, gate_bias, qk_conv, head_norm, pool_w, pool_scale, w_out, xattn_norm, mem_norm, xattn_wq, xattn_wkv, xattn_wo, ffn2_norm, ffn2_w_gate, ffn2_w_up, ffn2_w_down, final_norm):
    B, S, D = x.shape
    M = mem.shape[1]
    depth = ffn1_norm.shape[0]
    W = D // 2
    H = MLSTM_HEADS
    n_main = 5 * W

    bf = lambda w: w.astype(BF16)
    row = lambda v: v.reshape(v.shape[0], 1, v.shape[-1])

    w_in_t = jnp.swapaxes(w_in, 1, 2)
    lane_pad = lambda v: jnp.pad(v, [(0, 0)] * (v.ndim - 1) + [(0, LANES - H)])
    gate_cols = jnp.swapaxes(_gate_rows(w_in_t, n_main), 1, 2)
    w_gate = bf(jnp.concatenate([lane_pad(gate_cols[..., :H]), lane_pad(gate_cols[..., H:])], axis=-1))
    gbias = jnp.concatenate([lane_pad(gate_bias[:, :H]), lane_pad(gate_bias[:, H:])],
                            axis=-1).reshape(depth, 1, 2 * LANES)
    fg = final_norm.reshape(1, D)
    pool_rows = pool_w.reshape(depth, -1, pool_w.shape[-1])

    fgrid = _ffn_grid(B * S, ffn1_w_gate.shape[-1])

    def ffn_jobs(ws, l):
        return [_cast_job(w, l, fgrid) for w in ws], list(ws)

    xgrid = (B * S // _tile(S, XATTN_TM),)

    def xattn_jobs(l, last):
        srcs = [ffn2_w_gate, ffn2_w_up, ffn2_w_down]
        jobs = [_cast_job(w, l, xgrid) for w in srcs]
        if not last:
            srcs += [w_in_t, pool_rows, w_out, xattn_wq, xattn_wkv, xattn_wo]
            jobs += [_cast_job_t(w_in_t, l + 1, xgrid, n_main), _cast_job(pool_rows, l + 1, xgrid),
                     _cast_job(w_out, l + 1, xgrid, panels=True),
                     _cast_job(xattn_wq, l + 1, xgrid, panels=True),
                     _cast_job(xattn_wkv, l + 1, xgrid),
                     _cast_job(xattn_wo, l + 1, xgrid, panels=True)]
        return jobs, srcs

    fw = [bf(ffn1_w_gate[0]), bf(ffn1_w_up[0]), bf(ffn1_w_down[0])]
    mw = [None, bf(pool_rows[0]), _to_panels(bf(w_out[0])), _to_panels(bf(xattn_wq[0])),
          bf(xattn_wkv[0]), _to_panels(bf(xattn_wo[0]))]
    h = x.reshape(B * S, D)
    mem2 = mem.reshape(B * M, D)
    for l in range(depth):
        last = l == depth - 1
        h, _ = _ffn(h, row(ffn1_norm), fw[0], fw[1], fw[2], fg, l, [], [], final=False)
        if l == 0:
            n_steps = n_main // PANEL
            mw[0] = _copy_call(_cast_job_t(w_in_t, 0, (n_steps,), n_main), w_in_t, n_steps)
        w_main, poolw, wout, wq, wkv, wo = mw
        h = _mix(h, row(mix_norm), w_main, w_gate, gbias, qk_conv, row(head_norm),
                 poolw.reshape(pool_w.shape[1:]), row(pool_scale), wout, l, B=B, S=S)
        kv = _kv_proj(mem2, row(mem_norm), wkv, l)
        jobs, srcs = xattn_jobs(l, last)
        h, cw = _xattn(h, row(xattn_norm), wq, kv, wo, l, jobs, srcs, S=S, M=M)
        fw, mw = cw[:3], cw[3:]
        jobs, srcs = ([], []) if last else ffn_jobs((ffn1_w_gate, ffn1_w_up, ffn1_w_down), l + 1)
        h, cw = _ffn(h, row(ffn2_norm), fw[0], fw[1], fw[2], fg, l, jobs, srcs, final=last)
        fw = cw
    return h.reshape(B, S, D)
```

```python
import functools

import jax
import jax.numpy as jnp
from jax import lax
from jax.experimental import pallas as pl
from jax.experimental.pallas import tpu as pltpu

F32 = jnp.float32
BF16 = jnp.bfloat16

EPS = 1e-6
MLSTM_HEADS = 4
POOL_WINDOWS = (2, 4, 8, 16)
CONV_W = 4
XATTN_HEADS = 4
LANES = 128
BF16_ROWS = 16
PANEL = 512
POOL_HALO = 32
CONV_HALO = 8
MIB = 1024 * 1024

FFN_TM, FFN_TF, FFN_VMEM = 1024, 512, 60 * MIB
MIX_TILE, MIX_VMEM = 256, 56 * MIB
XATTN_TM, XATTN_VMEM = 512, 58 * MIB
KV_TM, KV_TN, KV_VMEM = 512, 1024, 32 * MIB


def _tile(n, pref):
    return pref if n % pref == 0 else n


def _rms(x, g):
    y = x * lax.rsqrt(jnp.mean(x * x, axis=-1, keepdims=True) + EPS)
    return y * g


def _silu(x):
    h = 0.5 * x
    return h * jnp.tanh(h) + h


def _log_sigmoid(x):
    return jnp.minimum(x, 0.0) - jnp.log1p(jnp.exp(-jnp.abs(x)))


def _params(vmem, *sem):
    return pltpu.CompilerParams(dimension_semantics=sem, vmem_limit_bytes=vmem)


_RESIDENT = dict(pipeline_mode=pl.Buffered(1))


def _panel_cols(ref, lo, n):
    p, off = divmod(lo, PANEL)
    assert off + n <= PANEL
    return ref[p, :, off:off + n]


def _to_panels(w):
    K, N = w.shape
    return w.reshape(K, N // PANEL, PANEL).transpose(1, 0, 2)


def _cast_job(w, layer, grid, panels=False):
    _, R, C = w.shape
    ni = grid[0]
    rep = 1
    while (R * rep) % (ni * BF16_ROWS) and rep < ni:
        rep *= 2
    nrb = ni // rep
    assert ni % rep == 0 and R % nrb == 0
    ncol = 1
    if len(grid) == 2:
        assert not panels
        ncol = max([n for n in range(2, grid[1] + 1) if C % (n * LANES) == 0] + [1])
        idx = lambda i, j: (i // rep, jnp.minimum(j, ncol - 1))
    else:
        idx = lambda i: (i // rep, 0)
    src = pl.BlockSpec((None, R // nrb, C // ncol), lambda *ij: (layer,) + idx(*ij))
    if panels:
        assert C % PANEL == 0
        dst = pl.BlockSpec((C // PANEL, R // nrb, PANEL), lambda i: (0, i // rep, 0))
        return src, dst, jax.ShapeDtypeStruct((C // PANEL, R, PANEL), BF16), "panels"
    dst = pl.BlockSpec((R // nrb, C // ncol), idx)
    return src, dst, jax.ShapeDtypeStruct((R, C), BF16), "plain"


def _cast_job_t(wt, layer, grid, ncols):
    R = wt.shape[-1]
    assert ncols % PANEL == 0
    per = pl.cdiv(ncols // LANES, grid[0])
    lpp = PANEL // LANES
    if per <= lpp:
        per = min(d for d in range(per, lpp + 1) if lpp % d == 0)
    else:
        per = lpp * pl.cdiv(per, lpp)
    cpb = per * LANES
    last = pl.cdiv(ncols, cpb) - 1
    blk = lambda *ij: jnp.minimum(ij[0], last)
    src = pl.BlockSpec((None, cpb, R), lambda *ij: (layer, blk(*ij), 0))
    if cpb <= PANEL:
        bpp = PANEL // cpb
        dst = pl.BlockSpec((None, R, cpb), lambda *ij: (blk(*ij) // bpp, 0, blk(*ij) % bpp))
    else:
        dst = pl.BlockSpec((cpb // PANEL, R, PANEL), lambda *ij: (blk(*ij), 0, 0))
    return src, dst, jax.ShapeDtypeStruct((ncols // PANEL, R, PANEL), BF16), "transposed"


def _copy_casts(src_refs, dst_refs, kinds):
    for src_ref, dst_ref, kind in zip(src_refs, dst_refs, kinds):
        v = src_ref[...]
        if kind == "transposed":
            v = v.T
        if dst_ref.ndim == 2:
            dst_ref[...] = v.astype(BF16)
        else:
            for p in range(dst_ref.shape[0]):
                dst_ref[p] = v[:, p * PANEL:(p + 1) * PANEL].astype(BF16)


def _copy_call(job, src, n_steps):
    def body(src_ref, dst_ref):
        _copy_casts([src_ref], [dst_ref], [job[3]])

    return pl.pallas_call(
        body, grid=(n_steps,), in_specs=[job[0]], out_specs=job[1], out_shape=job[2],
        compiler_params=_params(KV_VMEM, "arbitrary"), name="weight_copy")(src)


def _gate_rows(w_in_t, n_main):
    L, _, D = w_in_t.shape
    ng = 2 * MLSTM_HEADS
    assert n_main % ng == 0

    def body(src_ref, dst_ref):
        dst_ref[...] = src_ref[...]

    return pl.pallas_call(
        body, grid=(L,),
        in_specs=[pl.BlockSpec((None, ng, D), lambda l: (l, n_main // ng, 0))],
        out_specs=pl.BlockSpec((None, ng, D), lambda l: (l, 0, 0)),
        out_shape=jax.ShapeDtypeStruct((L, ng, D), F32),
        compiler_params=_params(KV_VMEM, "arbitrary"), name="gate_rows")(w_in_t)


def _ffn_kernel(*refs, final, kinds):
    nc = len(kinds)
    h_ref, g_ref, wg_ref, wu_ref, wd_ref, fg_ref = refs[:6]
    src_refs = refs[6:6 + nc]
    o_ref = refs[6 + nc]
    dst_refs = refs[7 + nc:7 + 2 * nc]
    xn_ref = refs[7 + 2 * nc]
    j = pl.program_id(1)

    def step(first):
        if first:
            x = h_ref[...]
            xn_ref[...] = _rms(x, g_ref[...]).astype(BF16)
        xn = xn_ref[...]
        a = jnp.dot(xn, wg_ref[...], preferred_element_type=F32)
        b = jnp.dot(xn, wu_ref[...], preferred_element_type=F32)
        mid = (_silu(a) * b).astype(BF16)
        base = h_ref[...] if first else o_ref[...]
        o_ref[...] = base + 0.5 * jnp.dot(mid, wd_ref[...], preferred_element_type=F32)
        _copy_casts(src_refs, dst_refs, kinds)

    pl.when(j == 0)(functools.partial(step, True))
    pl.when(j > 0)(functools.partial(step, False))

    if final:
        @pl.when(j == pl.num_programs(1) - 1)
        def _():
            o_ref[...] = _rms(o_ref[...], fg_ref[...])


def _ffn_grid(T, F):
    tm, tf = _tile(T, FFN_TM), _tile(F, FFN_TF)
    return T // tm, F // tf


def _ffn(h, g, wg, wu, wd, fg, layer, jobs, srcs, *, final):
    T, D = h.shape
    F = wg.shape[-1]
    ni, nj = _ffn_grid(T, F)
    tm, tf = T // ni, F // nj
    outs = pl.pallas_call(
        functools.partial(_ffn_kernel, final=final, kinds=tuple(jb[3] for jb in jobs)),
        grid=(ni, nj),
        in_specs=[
            pl.BlockSpec((tm, D), lambda i, j: (i, 0)),
            pl.BlockSpec((None, 1, D), lambda i, j: (layer, 0, 0)),
            pl.BlockSpec((D, tf), lambda i, j: (0, j)),
            pl.BlockSpec((D, tf), lambda i, j: (0, j)),
            pl.BlockSpec((tf, D), lambda i, j: (j, 0)),
            pl.BlockSpec((1, D), lambda i, j: (0, 0)),
        ] + [jb[0] for jb in jobs],
        out_specs=[pl.BlockSpec((tm, D), lambda i, j: (i, 0))] + [jb[1] for jb in jobs],
        out_shape=[jax.ShapeDtypeStruct((T, D), F32)] + [jb[2] for jb in jobs],
        scratch_shapes=[pltpu.VMEM((tm, D), BF16)],
        compiler_params=_params(FFN_VMEM, "arbitrary", "arbitrary"),
        name="ffn_final" if final else "ffn",
    )(h, g, wg, wu, wd, fg, *srcs)
    return outs[0], list(outs[1:])


def _scan_rows(x, op, fill, row):
    n = x.shape[0]
    sh = 1
    while sh < n:
        x = op(x, jnp.where(row >= sh, pltpu.roll(x, sh, axis=0), fill))
        sh *= 2
    return x


def _mix_kernel(h_ref, g_ref, win_ref, wgate_ref, gbias_ref, conv_ref, hn_ref, poolw_ref,
                pscale_ref, wout_ref,
                o_ref,
                pext_ref, s2_ref, s4_ref, cext_ref, c_ref, n_ref, m_ref,
                xn_ref, qs_ref, ks_ref, y_ref,
                *, ts, W):
    H = MLSTM_HEADS
    dh = W // H
    gw = W // len(POOL_WINDOWS)
    pw = min(PANEL, W)
    PH, CH = POOL_HALO, CONV_HALO
    s = pl.program_id(1)

    @pl.when(s == 0)
    def _():
        pext_ref[0:PH, :] = jnp.zeros((PH, W), F32)
        cext_ref[0:CH, :] = jnp.zeros((CH, 2 * W), F32)
        c_ref[...] = jnp.zeros_like(c_ref)
        n_ref[...] = jnp.zeros_like(n_ref)
        m_ref[...] = jnp.zeros_like(m_ref)

    x = h_ref[...]
    xn_ref[...] = _rms(x, g_ref[...]).astype(BF16)
    row = lax.broadcasted_iota(jnp.int32, (ts, 1), 0)

    def proj(lo, n):
        return jnp.dot(xn_ref[...], _panel_cols(win_ref, lo, n), preferred_element_type=F32)

    for c0 in range(0, W, pw):
        pext_ref[PH:PH + ts, c0:c0 + pw] = proj(c0, pw)
    tpos = row + s * ts + 1
    n_ext = PH + ts
    for g, win in enumerate(POOL_WINDOWS):
        cs = slice(g * gw, (g + 1) * gw)
        pg = pext_ref[PH:n_ext, cs]
        if win == 2:
            acc = pg + pext_ref[PH - 1:n_ext - 1, cs]
        else:
            s2_ref[8:n_ext, :] = pext_ref[8:n_ext, cs] + pext_ref[7:n_ext - 1, cs]
            if win == 4:
                acc = s2_ref[PH:n_ext, :] + s2_ref[PH - 2:n_ext - 2, :]
            else:
                s4_ref[16:n_ext, :] = s2_ref[16:n_ext, :] + s2_ref[14:n_ext - 2, :]
                if win == 8:
                    acc = s4_ref[PH:n_ext, :] + s4_ref[PH - 4:n_ext - 4, :]
                else:
                    s8 = s4_ref[24:n_ext, :] + s4_ref[20:n_ext - 4, :]
                    acc = s8[8:] + s8[:-8]
        inv_cnt = 1.0 / jnp.minimum(tpos, win).astype(F32)
        d = acc * inv_cnt - pg
        yg = jnp.dot(d.astype(BF16), poolw_ref[g], preferred_element_type=F32)
        y_ref[:, cs] = (yg * pscale_ref[:, cs]).astype(BF16)
    pext_ref[0:PH, :] = pext_ref[ts:ts + PH, :]

    nblk = W // dh
    for c0 in range(0, 2 * W, pw):
        cext_ref[CH:CH + ts, c0:c0 + pw] = proj(W + c0, pw)
    for blk in range(2 * nblk):
        hs = slice(blk * dh, (blk + 1) * dh)
        lo = CH - (CONV_W - 1)
        y = cext_ref[lo:lo + ts, hs] * conv_ref[0:1, hs]
        for j in range(1, CONV_W):
            y = y + cext_ref[lo + j:lo + j + ts, hs] * conv_ref[j:j + 1, hs]
        y = _silu(y)
        cext_ref[0:CH, hs] = cext_ref[ts:ts + CH, hs]
        if blk < nblk:
            qs_ref[:, hs] = (y * (dh ** -0.5)).astype(BF16)
        else:
            ks_ref[:, (blk - nblk) * dh:(blk - nblk + 1) * dh] = y

    gates = jnp.dot(xn_ref[...], wgate_ref[...], preferred_element_type=F32) + gbias_ref[...]
    li = gates[:, :LANES]
    lf = _log_sigmoid(gates[:, LANES:])
    b = _scan_rows(lf, jnp.add, 0.0, row)
    a = li - b
    m_prev = m_ref[...]
    u = jnp.maximum(m_prev, _scan_rows(a, jnp.maximum, -jnp.inf, row))
    u_l = u[ts - 1:ts, :]
    m_ref[...] = b[ts - 1:ts, :] + u_l
    iw = jnp.exp(m_prev - u)
    emt = jnp.exp(-b - u)
    ws = jnp.exp(a - u_l)
    decay = jnp.exp(m_prev - u_l)
    a_row = a.T
    tri = (lax.broadcasted_iota(jnp.int32, (ts, ts), 0)
           >= lax.broadcasted_iota(jnp.int32, (ts, ts), 1))

    for hd in range(H):
        cs = slice(hd * dh, (hd + 1) * dh)
        q_c = qs_ref[:, cs]
        k_c = ks_ref[:, cs]
        v_c = proj(3 * W + hd * dh, dh).astype(BF16)
        og = proj(4 * W + hd * dh, dh)
        sc = lax.dot_general(q_c, k_c.astype(BF16), (((1,), (1,)), ((), ())),
                             preferred_element_type=F32)
        e = jnp.where(tri, jnp.exp(a_row[hd:hd + 1, :] - u[:, hd:hd + 1]), 0.0)
        wm = e * sc
        cst = c_ref[hd]
        nst = n_ref[hd]
        iw_h = iw[:, hd:hd + 1]
        num = (iw_h * jnp.dot(q_c, cst.astype(BF16), preferred_element_type=F32)
               + jnp.dot(wm.astype(BF16), v_c, preferred_element_type=F32))
        den = (iw_h * jnp.sum(q_c.astype(F32) * nst, axis=-1, keepdims=True)
               + jnp.sum(wm, axis=-1, keepdims=True))
        hh = num * (1.0 / jnp.maximum(jnp.abs(den), emt[:, hd:hd + 1]))
        hh = hh * lax.rsqrt(jnp.mean(hh * hh, axis=-1, keepdims=True) + EPS)
        yv = hh * hn_ref[:, cs] * jax.nn.sigmoid(og)
        y_ref[:, W + hd * dh:W + (hd + 1) * dh] = yv.astype(BF16)
        kw = k_c * ws[:, hd:hd + 1]
        dec_h = decay[:, hd:hd + 1]
        c_ref[hd] = dec_h * cst + lax.dot_general(
            kw.astype(BF16), v_c, (((0,), (0,)), ((), ())), preferred_element_type=F32)
        n_ref[hd] = dec_h * nst + jnp.sum(kw, axis=0, keepdims=True)

    for p in range(wout_ref.shape[0]):
        cs = slice(p * PANEL, (p + 1) * PANEL)
        o_ref[:, cs] = x[:, cs] + jnp.dot(y_ref[...], wout_ref[p], preferred_element_type=F32)


def _mix(h, g, w_main, w_gate, gbias, conv_w, head_norm, pool_w, pool_scale, w_out, layer, *, B, S):
    T, D = h.shape
    W = D // 2
    H = MLSTM_HEADS
    dh = W // H
    ng = len(POOL_WINDOWS)
    gw = W // ng
    ts = _tile(S, MIX_TILE)
    nst = S // ts
    return pl.pallas_call(
        functools.partial(_mix_kernel, ts=ts, W=W),
        grid=(B, nst),
        in_specs=[
            pl.BlockSpec((ts, D), lambda b, s: (b * nst + s, 0)),
            pl.BlockSpec((None, 1, D), lambda b, s: (layer, 0, 0)),
            pl.BlockSpec(w_main.shape, lambda b, s: (0, 0, 0), **_RESIDENT),
            pl.BlockSpec((None, D, 2 * LANES), lambda b, s: (layer, 0, 0), **_RESIDENT),
            pl.BlockSpec((None, 1, 2 * LANES), lambda b, s: (layer, 0, 0)),
            pl.BlockSpec((None, CONV_W, 2 * W), lambda b, s: (layer, 0, 0)),
            pl.BlockSpec((None, 1, W), lambda b, s: (layer, 0, 0)),
            pl.BlockSpec((ng, gw, gw), lambda b, s: (0, 0, 0)),
            pl.BlockSpec((None, 1, W), lambda b, s: (layer, 0, 0)),
            pl.BlockSpec(w_out.shape, lambda b, s: (0, 0, 0), **_RESIDENT),
        ],
        out_specs=pl.BlockSpec((ts, D), lambda b, s: (b * nst + s, 0)),
        out_shape=jax.ShapeDtypeStruct((T, D), F32),
        scratch_shapes=[
            pltpu.VMEM((POOL_HALO + ts, W), F32),
            pltpu.VMEM((POOL_HALO + ts, gw), F32),
            pltpu.VMEM((POOL_HALO + ts, gw), F32),
            pltpu.VMEM((CONV_HALO + ts, 2 * W), F32),
            pltpu.VMEM((H, dh, dh), F32),
            pltpu.VMEM((H, 1, dh), F32),
            pltpu.VMEM((1, LANES), F32),
            pltpu.VMEM((ts, D), BF16),
            pltpu.VMEM((ts, W), BF16),
            pltpu.VMEM((ts, W), F32),
            pltpu.VMEM((ts, D), BF16),
        ],
        compiler_params=_params(MIX_VMEM, "arbitrary", "arbitrary"),
        name="mix",
    )(h, g, w_main, w_gate, gbias, conv_w, head_norm, pool_w, pool_scale, w_out)


def _kv_kernel(m_ref, g_ref, w_ref, o_ref, xn_ref):
    j = pl.program_id(1)

    @pl.when(j == 0)
    def _():
        xn_ref[...] = _rms(m_ref[...], g_ref[...]).astype(BF16)

    o_ref[...] = jnp.dot(xn_ref[...], w_ref[...], preferred_element_type=F32).astype(BF16)


def _kv_proj(mem, g, wkv, layer):
    T, D = mem.shape
    N = wkv.shape[-1]
    tm = _tile(T, KV_TM)
    tn = _tile(N, KV_TN)
    return pl.pallas_call(
        _kv_kernel,
        grid=(T // tm, N // tn),
        in_specs=[
            pl.BlockSpec((tm, D), lambda i, j: (i, 0)),
            pl.BlockSpec((None, 1, D), lambda i, j: (layer, 0, 0)),
            pl.BlockSpec((D, tn), lambda i, j: (0, j)),
        ],
        out_specs=pl.BlockSpec((tm, tn), lambda i, j: (i, j)),
        out_shape=jax.ShapeDtypeStruct((T, N), BF16),
        scratch_shapes=[pltpu.VMEM((tm, D), BF16)],
        compiler_params=_params(KV_VMEM, "parallel", "arbitrary"),
        name="xattn_kv",
    )(mem, g, wkv)


def _xattn_kernel(*refs, D, kinds):
    nc = len(kinds)
    h_ref, g_ref, wq_ref, kv_ref, wo_ref = refs[:5]
    o_ref = refs[5 + nc]
    xn_ref, ao_ref = refs[6 + 2 * nc:]
    H = XATTN_HEADS
    dh = D // H
    x = h_ref[...]
    xn_ref[...] = _rms(x, g_ref[...]).astype(BF16)
    for hd in range(H):
        cs = slice(hd * dh, (hd + 1) * dh)
        q = jnp.dot(xn_ref[...], _panel_cols(wq_ref, hd * dh, dh), preferred_element_type=F32)
        s = lax.dot_general(q.astype(BF16), kv_ref[:, cs], (((1,), (1,)), ((), ())),
                            preferred_element_type=F32) * (dh ** -0.5)
        e = jnp.exp(s - jnp.max(s, axis=-1, keepdims=True))
        pr = e * (1.0 / jnp.sum(e, axis=-1, keepdims=True))
        ao_ref[:, cs] = jnp.dot(pr.astype(BF16), kv_ref[:, D + hd * dh:D + (hd + 1) * dh],
                                preferred_element_type=F32).astype(BF16)
    for p in range(wo_ref.shape[0]):
        cs = slice(p * PANEL, (p + 1) * PANEL)
        o_ref[:, cs] = x[:, cs] + jnp.dot(ao_ref[...], wo_ref[p], preferred_element_type=F32)
    _copy_casts(refs[5:5 + nc], refs[6 + nc:6 + 2 * nc], kinds)


def _xattn(h, g, wq, kv, wo, layer, jobs, srcs, *, S, M):
    T, D = h.shape
    tm = _tile(S, XATTN_TM)
    nsb = S // tm
    outs = pl.pallas_call(
        functools.partial(_xattn_kernel, D=D, kinds=tuple(jb[3] for jb in jobs)),
        grid=(T // tm,),
        in_specs=[
            pl.BlockSpec((tm, D), lambda i: (i, 0)),
            pl.BlockSpec((None, 1, D), lambda i: (layer, 0, 0)),
            pl.BlockSpec(wq.shape, lambda i: (0, 0, 0), **_RESIDENT),
            pl.BlockSpec((M, 2 * D), lambda i: (i // nsb, 0)),
            pl.BlockSpec(wo.shape, lambda i: (0, 0, 0), **_RESIDENT),
        ] + [jb[0] for jb in jobs],
        out_specs=[pl.BlockSpec((tm, D), lambda i: (i, 0))] + [jb[1] for jb in jobs],
        out_shape=[jax.ShapeDtypeStruct((T, D), F32)] + [jb[2] for jb in jobs],
        scratch_shapes=[pltpu.VMEM((tm, D), BF16), pltpu.VMEM((tm, D), BF16)],
        compiler_params=_params(XATTN_VMEM, "arbitrary"),
        name="xattn",
    )(h, g, wq, kv, wo, *srcs)
    return outs[0], list(outs[1:])


def kernel(x, mem, ffn1_norm, ffn1_w_gate, ffn1_w_up, ffn1_w_down, mix_norm, w_in, gate_bias, qk_conv, head_norm, pool_w, pool_scale, w_out, xattn_norm, mem_norm, xattn_wq, xattn_wkv, xattn_wo, ffn2_norm, ffn2_w_gate, ffn2_w_up, ffn2_w_down, final_norm):
    B, S, D = x.shape
    M = mem.shape[1]
    depth = ffn1_norm.shape[0]
    W = D // 2
    H = MLSTM_HEADS
    n_main = 5 * W

    bf = lambda w: w.astype(BF16)
    row = lambda v: v.reshape(v.shape[0], 1, v.shape[-1])

    w_in_t = jnp.swapaxes(w_in, 1, 2)
    lane_pad = lambda v: jnp.pad(v, [(0, 0)] * (v.ndim - 1) + [(0, LANES - H)])
    gate_cols = jnp.swapaxes(_gate_rows(w_in_t, n_main), 1, 2)
    w_gate = bf(jnp.concatenate([lane_pad(gate_cols[..., :H]), lane_pad(gate_cols[..., H:])], axis=-1))
    gbias = jnp.concatenate([lane_pad(gate_bias[:, :H]), lane_pad(gate_bias[:, H:])],
                            axis=-1).reshape(depth, 1, 2 * LANES)
    fg = final_norm.reshape(1, D)
    pool_rows = pool_w.reshape(depth, -1, pool_w.shape[-1])

    fgrid = _ffn_grid(B * S, ffn1_w_gate.shape[-1])

    def ffn_jobs(ws, l):
        return [_cast_job(w, l, fgrid) for w in ws], list(ws)

    xgrid = (B * S // _tile(S, XATTN_TM),)

    def xattn_jobs(l, last):
        srcs = [ffn2_w_gate, ffn2_w_up, ffn2_w_down]
        jobs = [_cast_job(w, l, xgrid) for w in srcs]
        if not last:
            srcs += [w_in_t, pool_rows, w_out, xattn_wq, xattn_wkv, xattn_wo]
            jobs += [_cast_job_t(w_in_t, l + 1, xgrid, n_main), _cast_job(pool_rows, l + 1, xgrid),
                     _cast_job(w_out, l + 1, xgrid, panels=True),
                     _cast_job(xattn_wq, l + 1, xgrid, panels=True),
                     _cast_job(xattn_wkv, l + 1, xgrid),
                     _cast_job(xattn_wo, l + 1, xgrid, panels=True)]
        return jobs, srcs

    fw = [bf(ffn1_w_gate[0]), bf(ffn1_w_up[0]), bf(ffn1_w_down[0])]
    mw = [None, bf(pool_rows[0]), _to_panels(bf(w_out[0])), _to_panels(bf(xattn_wq[0])),
          bf(xattn_wkv[0]), _to_panels(bf(xattn_wo[0]))]
    h = x.reshape(B * S, D)
    mem2 = mem.reshape(B * M, D)
    for l in range(depth):
        last = l == depth - 1
        h, _ = _ffn(h, row(ffn1_norm), fw[0], fw[1], fw[2], fg, l, [], [], final=False)
        if l == 0:
            n_steps = n_main // PANEL
            mw[0] = _copy_call(_cast_job_t(w_in_t, 0, (n_steps,), n_main), w_in_t, n_steps)
        w_main, poolw, wout, wq, wkv, wo = mw
        h = _mix(h, row(mix_norm), w_main, w_gate, gbias, qk_conv, row(head_norm),
                 poolw.reshape(pool_w.shape[1:]), row(pool_scale), wout, l, B=B, S=S)
        kv = _kv_proj(mem2, row(mem_norm), wkv, l)
        jobs, srcs = xattn_jobs(l, last)
        h, cw = _xattn(h, row(xattn_norm), wq, kv, wo, l, jobs, srcs, S=S, M=M)
        fw, mw = cw[:3], cw[3:]
        jobs, srcs = ([], []) if last else ffn_jobs((ffn1_w_gate, ffn1_w_up, ffn1_w_down), l + 1)
        h, cw = _ffn(h, row(ffn2_norm), fw[0], fw[1], fw[2], fg, l, jobs, srcs, final=last)
        fw = cw
    return h.reshape(B, S, D)
```

```python
import functools

import jax
import jax.numpy as jnp
from jax import lax
from jax.experimental import pallas as pl
from jax.experimental.pallas import tpu as pltpu

F32 = jnp.float32
BF16 = jnp.bfloat16

EPS = 1e-6
MLSTM_HEADS = 4
POOL_WINDOWS = (2, 4, 8, 16)
CONV_W = 4
XATTN_HEADS = 4
LANES = 128
BF16_ROWS = 16
PANEL = 512
CAST_COLS = 4
POOL_HALO = 32
CONV_HALO = 8
MIB = 1024 * 1024

FFN_TM, FFN_TF, FFN_VMEM = 1024, 512, 60 * MIB
MIX_TILE, MIX_VMEM = 256, 56 * MIB
XATTN_TM, XATTN_VMEM = 512, 58 * MIB
KV_TM, KV_TN, KV_VMEM = 512, 1024, 32 * MIB


def _tile(n, pref):
    return pref if n % pref == 0 else n


def _rms(x, g):
    y = x * lax.rsqrt(jnp.mean(x * x, axis=-1, keepdims=True) + EPS)
    return y * g


def _silu(x):
    h = 0.5 * x
    return h * jnp.tanh(h) + h


def _log_sigmoid(x):
    return jnp.minimum(x, 0.0) - jnp.log1p(jnp.exp(-jnp.abs(x)))


def _params(vmem, *sem):
    return pltpu.CompilerParams(dimension_semantics=sem, vmem_limit_bytes=vmem)


_RESIDENT = dict(pipeline_mode=pl.Buffered(1))


def _panel_cols(ref, lo, n):
    p, off = divmod(lo, PANEL)
    assert off + n <= PANEL
    return ref[p, :, off:off + n]


def _to_panels(w):
    K, N = w.shape
    return w.reshape(K, N // PANEL, PANEL).transpose(1, 0, 2)


def _cast_job(w, layer, grid, panels=False):
    _, R, C = w.shape
    ni = grid[0]
    rep = 1
    while (R * rep) % (ni * BF16_ROWS) and rep < ni:
        rep *= 2
    nrb = ni // rep
    assert ni % rep == 0 and R % nrb == 0
    ncol = 1
    if len(grid) == 2:
        assert not panels
        ncol = max([n for n in range(2, min(grid[1], CAST_COLS) + 1) if C % (n * LANES) == 0] + [1])
        idx = lambda i, j: (i // rep, jnp.minimum(j, ncol - 1))
    else:
        idx = lambda i: (i // rep, 0)
    src = pl.BlockSpec((None, R // nrb, C // ncol), lambda *ij: (layer,) + idx(*ij))
    if panels:
        assert C % PANEL == 0
        dst = pl.BlockSpec((C // PANEL, R // nrb, PANEL), lambda i: (0, i // rep, 0))
        return src, dst, jax.ShapeDtypeStruct((C // PANEL, R, PANEL), BF16), "panels"
    dst = pl.BlockSpec((R // nrb, C // ncol), idx)
    return src, dst, jax.ShapeDtypeStruct((R, C), BF16), "plain"


def _cast_job_t(wt, layer, grid, ncols):
    R = wt.shape[-1]
    assert ncols % PANEL == 0
    per = pl.cdiv(ncols // LANES, grid[0])
    lpp = PANEL // LANES
    if per <= lpp:
        per = min(d for d in range(per, lpp + 1) if lpp % d == 0)
    else:
        per = lpp * pl.cdiv(per, lpp)
    cpb = per * LANES
    last = pl.cdiv(ncols, cpb) - 1
    blk = lambda *ij: jnp.minimum(ij[0], last)
    src = pl.BlockSpec((None, cpb, R), lambda *ij: (layer, blk(*ij), 0))
    if cpb <= PANEL:
        bpp = PANEL // cpb
        dst = pl.BlockSpec((None, R, cpb), lambda *ij: (blk(*ij) // bpp, 0, blk(*ij) % bpp))
    else:
        dst = pl.BlockSpec((cpb // PANEL, R, PANEL), lambda *ij: (blk(*ij), 0, 0))
    return src, dst, jax.ShapeDtypeStruct((ncols // PANEL, R, PANEL), BF16), "transposed"


def _copy_casts(src_refs, dst_refs, kinds):
    for src_ref, dst_ref, kind in zip(src_refs, dst_refs, kinds):
        v = src_ref[...]
        if kind == "transposed":
            v = v.T
        if dst_ref.ndim == 2:
            dst_ref[...] = v.astype(BF16)
        else:
            for p in range(dst_ref.shape[0]):
                dst_ref[p] = v[:, p * PANEL:(p + 1) * PANEL].astype(BF16)


def _copy_call(job, src, n_steps):
    def body(src_ref, dst_ref):
        _copy_casts([src_ref], [dst_ref], [job[3]])

    return pl.pallas_call(
        body, grid=(n_steps,), in_specs=[job[0]], out_specs=job[1], out_shape=job[2],
        compiler_params=_params(KV_VMEM, "arbitrary"), name="weight_copy")(src)


def _gate_rows(w_in_t, n_main):
    L, _, D = w_in_t.shape
    ng = 2 * MLSTM_HEADS
    assert n_main % ng == 0

    def body(src_ref, dst_ref):
        dst_ref[...] = src_ref[...]

    return pl.pallas_call(
        body, grid=(L,),
        in_specs=[pl.BlockSpec((None, ng, D), lambda l: (l, n_main // ng, 0))],
        out_specs=pl.BlockSpec((None, ng, D), lambda l: (l, 0, 0)),
        out_shape=jax.ShapeDtypeStruct((L, ng, D), F32),
        compiler_params=_params(KV_VMEM, "arbitrary"), name="gate_rows")(w_in_t)


def _ffn_kernel(*refs, final, kinds):
    nc = len(kinds)
    h_ref, g_ref, wg_ref, wu_ref, wd_ref, fg_ref = refs[:6]
    src_refs = refs[6:6 + nc]
    o_ref = refs[6 + nc]
    dst_refs = refs[7 + nc:7 + 2 * nc]
    xn_ref = refs[7 + 2 * nc]
    j = pl.program_id(1)

    def step(first):
        if first:
            x = h_ref[...]
            xn_ref[...] = _rms(x, g_ref[...]).astype(BF16)
        xn = xn_ref[...]
        a = jnp.dot(xn, wg_ref[...], preferred_element_type=F32)
        b = jnp.dot(xn, wu_ref[...], preferred_element_type=F32)
        mid = (_silu(a) * b).astype(BF16)
        base = h_ref[...] if first else o_ref[...]
        o_ref[...] = base + 0.5 * jnp.dot(mid, wd_ref[...], preferred_element_type=F32)
        _copy_casts(src_refs, dst_refs, kinds)

    pl.when(j == 0)(functools.partial(step, True))
    pl.when(j > 0)(functools.partial(step, False))

    if final:
        @pl.when(j == pl.num_programs(1) - 1)
        def _():
            o_ref[...] = _rms(o_ref[...], fg_ref[...])


def _ffn_grid(T, F):
    tm, tf = _tile(T, FFN_TM), _tile(F, FFN_TF)
    return T // tm, F // tf


def _ffn(h, g, wg, wu, wd, fg, layer, jobs, srcs, *, final):
    T, D = h.shape
    F = wg.shape[-1]
    ni, nj = _ffn_grid(T, F)
    tm, tf = T // ni, F // nj
    outs = pl.pallas_call(
        functools.partial(_ffn_kernel, final=final, kinds=tuple(jb[3] for jb in jobs)),
        grid=(ni, nj),
        in_specs=[
            pl.BlockSpec((tm, D), lambda i, j: (i, 0)),
            pl.BlockSpec((None, 1, D), lambda i, j: (layer, 0, 0)),
            pl.BlockSpec((D, tf), lambda i, j: (0, j)),
            pl.BlockSpec((D, tf), lambda i, j: (0, j)),
            pl.BlockSpec((tf, D), lambda i, j: (j, 0)),
            pl.BlockSpec((1, D), lambda i, j: (0, 0)),
        ] + [jb[0] for jb in jobs],
        out_specs=[pl.BlockSpec((tm, D), lambda i, j: (i, 0))] + [jb[1] for jb in jobs],
        out_shape=[jax.ShapeDtypeStruct((T, D), F32)] + [jb[2] for jb in jobs],
        scratch_shapes=[pltpu.VMEM((tm, D), BF16)],
        compiler_params=_params(FFN_VMEM, "arbitrary", "arbitrary"),
        name="ffn_final" if final else "ffn",
    )(h, g, wg, wu, wd, fg, *srcs)
    return outs[0], list(outs[1:])


def _scan_rows(x, op, fill, row):
    n = x.shape[0]
    sh = 1
    while sh < n:
        x = op(x, jnp.where(row >= sh, pltpu.roll(x, sh, axis=0), fill))
        sh *= 2
    return x


def _mix_kernel(h_ref, g_ref, win_ref, wgate_ref, gbias_ref, conv_ref, hn_ref, poolw_ref,
                pscale_ref, wout_ref,
                o_ref,
                pext_ref, s2_ref, s4_ref, cext_ref, c_ref, n_ref, m_ref,
                xn_ref, qs_ref, ks_ref, y_ref,
                *, ts, W):
    H = MLSTM_HEADS
    dh = W // H
    gw = W // len(POOL_WINDOWS)
    pw = min(PANEL, W)
    PH, CH = POOL_HALO, CONV_HALO
    s = pl.program_id(1)

    @pl.when(s == 0)
    def _():
        pext_ref[0:PH, :] = jnp.zeros((PH, W), F32)
        cext_ref[0:CH, :] = jnp.zeros((CH, 2 * W), F32)
        c_ref[...] = jnp.zeros_like(c_ref)
        n_ref[...] = jnp.zeros_like(n_ref)
        m_ref[...] = jnp.zeros_like(m_ref)

    x = h_ref[...]
    xn_ref[...] = _rms(x, g_ref[...]).astype(BF16)
    row = lax.broadcasted_iota(jnp.int32, (ts, 1), 0)

    def proj(lo, n):
        return jnp.dot(xn_ref[...], _panel_cols(win_ref, lo, n), preferred_element_type=F32)

    for c0 in range(0, W, pw):
        pext_ref[PH:PH + ts, c0:c0 + pw] = proj(c0, pw)
    tpos = row + s * ts + 1
    n_ext = PH + ts
    for g, win in enumerate(POOL_WINDOWS):
        cs = slice(g * gw, (g + 1) * gw)
        pg = pext_ref[PH:n_ext, cs]
        if win == 2:
            acc = pg + pext_ref[PH - 1:n_ext - 1, cs]
        else:
            s2_ref[8:n_ext, :] = pext_ref[8:n_ext, cs] + pext_ref[7:n_ext - 1, cs]
            if win == 4:
                acc = s2_ref[PH:n_ext, :] + s2_ref[PH - 2:n_ext - 2, :]
            else:
                s4_ref[16:n_ext, :] = s2_ref[16:n_ext, :] + s2_ref[14:n_ext - 2, :]
                if win == 8:
                    acc = s4_ref[PH:n_ext, :] + s4_ref[PH - 4:n_ext - 4, :]
                else:
                    s8 = s4_ref[24:n_ext, :] + s4_ref[20:n_ext - 4, :]
                    acc = s8[8:] + s8[:-8]
        inv_cnt = 1.0 / jnp.minimum(tpos, win).astype(F32)
        d = acc * inv_cnt - pg
        yg = jnp.dot(d.astype(BF16), poolw_ref[g], preferred_element_type=F32)
        y_ref[:, cs] = (yg * pscale_ref[:, cs]).astype(BF16)
    pext_ref[0:PH, :] = pext_ref[ts:ts + PH, :]

    nblk = W // dh
    for c0 in range(0, 2 * W, pw):
        cext_ref[CH:CH + ts, c0:c0 + pw] = proj(W + c0, pw)
    for blk in range(2 * nblk):
        hs = slice(blk * dh, (blk + 1) * dh)
        lo = CH - (CONV_W - 1)
        y = cext_ref[lo:lo + ts, hs] * conv_ref[0:1, hs]
        for j in range(1, CONV_W):
            y = y + cext_ref[lo + j:lo + j + ts, hs] * conv_ref[j:j + 1, hs]
        y = _silu(y)
        cext_ref[0:CH, hs] = cext_ref[ts:ts + CH, hs]
        if blk < nblk:
            qs_ref[:, hs] = (y * (dh ** -0.5)).astype(BF16)
        else:
            ks_ref[:, (blk - nblk) * dh:(blk - nblk + 1) * dh] = y

    gates = jnp.dot(xn_ref[...], wgate_ref[...], preferred_element_type=F32) + gbias_ref[...]
    li = gates[:, :LANES]
    lf = _log_sigmoid(gates[:, LANES:])
    b = _scan_rows(lf, jnp.add, 0.0, row)
    a = li - b
    m_prev = m_ref[...]
    u = jnp.maximum(m_prev, _scan_rows(a, jnp.maximum, -jnp.inf, row))
    u_l = u[ts - 1:ts, :]
    m_ref[...] = b[ts - 1:ts, :] + u_l
    iw = jnp.exp(m_prev - u)
    emt = jnp.exp(-b - u)
    ws = jnp.exp(a - u_l)
    decay = jnp.exp(m_prev - u_l)
    a_row = a.T
    tri = (lax.broadcasted_iota(jnp.int32, (ts, ts), 0)
           >= lax.broadcasted_iota(jnp.int32, (ts, ts), 1))

    for hd in range(H):
        cs = slice(hd * dh, (hd + 1) * dh)
        q_c = qs_ref[:, cs]
        k_c = ks_ref[:, cs]
        v_c = proj(3 * W + hd * dh, dh).astype(BF16)
        og = proj(4 * W + hd * dh, dh)
        sc = lax.dot_general(q_c, k_c.astype(BF16), (((1,), (1,)), ((), ())),
                             preferred_element_type=F32)
        e = jnp.where(tri, jnp.exp(a_row[hd:hd + 1, :] - u[:, hd:hd + 1]), 0.0)
        wm = e * sc
        cst = c_ref[hd]
        nst = n_ref[hd]
        iw_h = iw[:, hd:hd + 1]
        num = (iw_h * jnp.dot(q_c, cst.astype(BF16), preferred_element_type=F32)
               + jnp.dot(wm.astype(BF16), v_c, preferred_element_type=F32))
        den = (iw_h * jnp.sum(q_c.astype(F32) * nst, axis=-1, keepdims=True)
               + jnp.sum(wm, axis=-1, keepdims=True))
        hh = num * (1.0 / jnp.maximum(jnp.abs(den), emt[:, hd:hd + 1]))
        hh = hh * lax.rsqrt(jnp.mean(hh * hh, axis=-1, keepdims=True) + EPS)
        yv = hh * hn_ref[:, cs] * jax.nn.sigmoid(og)
        y_ref[:, W + hd * dh:W + (hd + 1) * dh] = yv.astype(BF16)
        kw = k_c * ws[:, hd:hd + 1]
        dec_h = decay[:, hd:hd + 1]
        c_ref[hd] = dec_h * cst + lax.dot_general(
            kw.astype(BF16), v_c, (((0,), (0,)), ((), ())), preferred_element_type=F32)
        n_ref[hd] = dec_h * nst + jnp.sum(kw, axis=0, keepdims=True)

    for p in range(wout_ref.shape[0]):
        cs = slice(p * PANEL, (p + 1) * PANEL)
        o_ref[:, cs] = x[:, cs] + jnp.dot(y_ref[...], wout_ref[p], preferred_element_type=F32)


def _mix(h, g, w_main, w_gate, gbias, conv_w, head_norm, pool_w, pool_scale, w_out, layer, *, B, S):
    T, D = h.shape
    W = D // 2
    H = MLSTM_HEADS
    dh = W // H
    ng = len(POOL_WINDOWS)
    gw = W // ng
    ts = _tile(S, MIX_TILE)
    nst = S // ts
    return pl.pallas_call(
        functools.partial(_mix_kernel, ts=ts, W=W),
        grid=(B, nst),
        in_specs=[
            pl.BlockSpec((ts, D), lambda b, s: (b * nst + s, 0)),
            pl.BlockSpec((None, 1, D), lambda b, s: (layer, 0, 0)),
            pl.BlockSpec(w_main.shape, lambda b, s: (0, 0, 0), **_RESIDENT),
            pl.BlockSpec((None, D, 2 * LANES), lambda b, s: (layer, 0, 0), **_RESIDENT),
            pl.BlockSpec((None, 1, 2 * LANES), lambda b, s: (layer, 0, 0)),
            pl.BlockSpec((None, CONV_W, 2 * W), lambda b, s: (layer, 0, 0)),
            pl.BlockSpec((None, 1, W), lambda b, s: (layer, 0, 0)),
            pl.BlockSpec((ng, gw, gw), lambda b, s: (0, 0, 0)),
            pl.BlockSpec((None, 1, W), lambda b, s: (layer, 0, 0)),
            pl.BlockSpec(w_out.shape, lambda b, s: (0, 0, 0), **_RESIDENT),
        ],
        out_specs=pl.BlockSpec((ts, D), lambda b, s: (b * nst + s, 0)),
        out_shape=jax.ShapeDtypeStruct((T, D), F32),
        scratch_shapes=[
            pltpu.VMEM((POOL_HALO + ts, W), F32),
            pltpu.VMEM((POOL_HALO + ts, gw), F32),
            pltpu.VMEM((POOL_HALO + ts, gw), F32),
            pltpu.VMEM((CONV_HALO + ts, 2 * W), F32),
            pltpu.VMEM((H, dh, dh), F32),
            pltpu.VMEM((H, 1, dh), F32),
            pltpu.VMEM((1, LANES), F32),
            pltpu.VMEM((ts, D), BF16),
            pltpu.VMEM((ts, W), BF16),
            pltpu.VMEM((ts, W), F32),
            pltpu.VMEM((ts, D), BF16),
        ],
        compiler_params=_params(MIX_VMEM, "arbitrary", "arbitrary"),
        name="mix",
    )(h, g, w_main, w_gate, gbias, conv_w, head_norm, pool_w, pool_scale, w_out)


def _kv_kernel(m_ref, g_ref, w_ref, o_ref, xn_ref):
    j = pl.program_id(1)

    @pl.when(j == 0)
    def _():
        xn_ref[...] = _rms(m_ref[...], g_ref[...]).astype(BF16)

    o_ref[...] = jnp.dot(xn_ref[...], w_ref[...], preferred_element_type=F32).astype(BF16)


def _kv_proj(mem, g, wkv, layer):
    T, D = mem.shape
    N = wkv.shape[-1]
    tm = _tile(T, KV_TM)
    tn = _tile(N, KV_TN)
    return pl.pallas_call(
        _kv_kernel,
        grid=(T // tm, N // tn),
        in_specs=[
            pl.BlockSpec((tm, D), lambda i, j: (i, 0)),
            pl.BlockSpec((None, 1, D), lambda i, j: (layer, 0, 0)),
            pl.BlockSpec((D, tn), lambda i, j: (0, j)),
        ],
        out_specs=pl.BlockSpec((tm, tn), lambda i, j: (i, j)),
        out_shape=jax.ShapeDtypeStruct((T, N), BF16),
        scratch_shapes=[pltpu.VMEM((tm, D), BF16)],
        compiler_params=_params(KV_VMEM, "parallel", "arbitrary"),
        name="xattn_kv",
    )(mem, g, wkv)


def _xattn_kernel(*refs, D, kinds):
    nc = len(kinds)
    h_ref, g_ref, wq_ref, kv_ref, wo_ref = refs[:5]
    o_ref = refs[5 + nc]
    xn_ref, ao_ref = refs[6 + 2 * nc:]
    H = XATTN_HEADS
    dh = D // H
    x = h_ref[...]
    xn_ref[...] = _rms(x, g_ref[...]).astype(BF16)
    for hd in range(H):
        cs = slice(hd * dh, (hd + 1) * dh)
        q = jnp.dot(xn_ref[...], _panel_cols(wq_ref, hd * dh, dh), preferred_element_type=F32)
        s = lax.dot_general(q.astype(BF16), kv_ref[:, cs], (((1,), (1,)), ((), ())),
                            preferred_element_type=F32) * (dh ** -0.5)
        e = jnp.exp(s - jnp.max(s, axis=-1, keepdims=True))
        pr = e * (1.0 / jnp.sum(e, axis=-1, keepdims=True))
        ao_ref[:, cs] = jnp.dot(pr.astype(BF16), kv_ref[:, D + hd * dh:D + (hd + 1) * dh],
                                preferred_element_type=F32).astype(BF16)
    for p in range(wo_ref.shape[0]):
        cs = slice(p * PANEL, (p + 1) * PANEL)
        o_ref[:, cs] = x[:, cs] + jnp.dot(ao_ref[...], wo_ref[p], preferred_element_type=F32)
    _copy_casts(refs[5:5 + nc], refs[6 + nc:6 + 2 * nc], kinds)


def _xattn(h, g, wq, kv, wo, layer, jobs, srcs, *, S, M):
    T, D = h.shape
    tm = _tile(S, XATTN_TM)
    nsb = S // tm
    outs = pl.pallas_call(
        functools.partial(_xattn_kernel, D=D, kinds=tuple(jb[3] for jb in jobs)),
        grid=(T // tm,),
        in_specs=[
            pl.BlockSpec((tm, D), lambda i: (i, 0)),
            pl.BlockSpec((None, 1, D), lambda i: (layer, 0, 0)),
            pl.BlockSpec(wq.shape, lambda i: (0, 0, 0), **_RESIDENT),
            pl.BlockSpec((M, 2 * D), lambda i: (i // nsb, 0)),
            pl.BlockSpec(wo.shape, lambda i: (0, 0, 0), **_RESIDENT),
        ] + [jb[0] for jb in jobs],
        out_specs=[pl.BlockSpec((tm, D), lambda i: (i, 0))] + [jb[1] for jb in jobs],
        out_shape=[jax.ShapeDtypeStruct((T, D), F32)] + [jb[2] for jb in jobs],
        scratch_shapes=[pltpu.VMEM((tm, D), BF16), pltpu.VMEM((tm, D), BF16)],
        compiler_params=_params(XATTN_VMEM, "arbitrary"),
        name="xattn",
    )(h, g, wq, kv, wo, *srcs)
    return outs[0], list(outs[1:])


def kernel(x, mem, ffn1_norm, ffn1_w_gate, ffn1_w_up, ffn1_w_down, mix_norm, w_in, gate_bias, qk_conv, head_norm, pool_w, pool_scale, w_out, xattn_norm, mem_norm, xattn_wq, xattn_wkv, xattn_wo, ffn2_norm, ffn2_w_gate, ffn2_w_up, ffn2_w_down, final_norm):
    B, S, D = x.shape
    M = mem.shape[1]
    depth = ffn1_norm.shape[0]
    W = D // 2
    H = MLSTM_HEADS
    n_main = 5 * W

    bf = lambda w: w.astype(BF16)
    row = lambda v: v.reshape(v.shape[0], 1, v.shape[-1])

    w_in_t = jnp.swapaxes(w_in, 1, 2)
    lane_pad = lambda v: jnp.pad(v, [(0, 0)] * (v.ndim - 1) + [(0, LANES - H)])
    gate_cols = jnp.swapaxes(_gate_rows(w_in_t, n_main), 1, 2)
    w_gate = bf(jnp.concatenate([lane_pad(gate_cols[..., :H]), lane_pad(gate_cols[..., H:])], axis=-1))
    gbias = jnp.concatenate([lane_pad(gate_bias[:, :H]), lane_pad(gate_bias[:, H:])],
                            axis=-1).reshape(depth, 1, 2 * LANES)
    fg = final_norm.reshape(1, D)
    pool_rows = pool_w.reshape(depth, -1, pool_w.shape[-1])

    fgrid = _ffn_grid(B * S, ffn1_w_gate.shape[-1])

    def ffn_jobs(ws, l):
        return [_cast_job(w, l, fgrid) for w in ws], list(ws)

    xgrid = (B * S // _tile(S, XATTN_TM),)

    def xattn_jobs(l, last):
        srcs = [ffn2_w_gate, ffn2_w_up, ffn2_w_down]
        jobs = [_cast_job(w, l, xgrid) for w in srcs]
        if not last:
            srcs += [w_in_t, pool_rows, w_out, xattn_wq, xattn_wkv, xattn_wo]
            jobs += [_cast_job_t(w_in_t, l + 1, xgrid, n_main), _cast_job(pool_rows, l + 1, xgrid),
                     _cast_job(w_out, l + 1, xgrid, panels=True),
                     _cast_job(xattn_wq, l + 1, xgrid, panels=True),
                     _cast_job(xattn_wkv, l + 1, xgrid),
                     _cast_job(xattn_wo, l + 1, xgrid, panels=True)]
        return jobs, srcs

    fw = [bf(ffn1_w_gate[0]), bf(ffn1_w_up[0]), bf(ffn1_w_down[0])]
    mw = [None, bf(pool_rows[0]), _to_panels(bf(w_out[0])), _to_panels(bf(xattn_wq[0])),
          bf(xattn_wkv[0]), _to_panels(bf(xattn_wo[0]))]
    h = x.reshape(B * S, D)
    mem2 = mem.reshape(B * M, D)
    for l in range(depth):
        last = l == depth - 1
        h, _ = _ffn(h, row(ffn1_norm), fw[0], fw[1], fw[2], fg, l, [], [], final=False)
        if l == 0:
            n_steps = n_main // PANEL
            mw[0] = _copy_call(_cast_job_t(w_in_t, 0, (n_steps,), n_main), w_in_t, n_steps)
        w_main, poolw, wout, wq, wkv, wo = mw
        h = _mix(h, row(mix_norm), w_main, w_gate, gbias, qk_conv, row(head_norm),
                 poolw.reshape(pool_w.shape[1:]), row(pool_scale), wout, l, B=B, S=S)
        kv = _kv_proj(mem2, row(mem_norm), wkv, l)
        jobs, srcs = xattn_jobs(l, last)
        h, cw = _xattn(h, row(xattn_norm), wq, kv, wo, l, jobs, srcs, S=S, M=M)
        fw, mw = cw[:3], cw[3:]
        jobs, srcs = ([], []) if last else ffn_jobs((ffn1_w_gate, ffn1_w_up, ffn1_w_down), l + 1)
        h, cw = _ffn(h, row(ffn2_norm), fw[0], fw[1], fw[2], fg, l, jobs, srcs, final=last)
        fw = cw
    return h.reshape(B, S, D)
```

```python
import functools

import jax
import jax.numpy as jnp
from jax import lax
from jax.experimental import pallas as pl
from jax.experimental.pallas import tpu as pltpu

F32 = jnp.float32
BF16 = jnp.bfloat16

EPS = 1e-6
MLSTM_HEADS = 4
POOL_WINDOWS = (2, 4, 8, 16)
CONV_W = 4
XATTN_HEADS = 4
LANES = 128
BF16_ROWS = 16
PANEL = 512
POOL_HALO = 32
CONV_HALO = 8
MIB = 1024 * 1024

FFN_TM, FFN_TF, FFN_VMEM = 1024, 512, 60 * MIB
MIX_TILE, MIX_VMEM = 256, 56 * MIB
XATTN_TM, XATTN_VMEM = 512, 58 * MIB
KV_TM, KV_TN, KV_VMEM = 512, 1024, 32 * MIB


def _tile(n, pref):
    return pref if n % pref == 0 else n


def _rms(x, g):
    y = x * lax.rsqrt(jnp.mean(x * x, axis=-1, keepdims=True) + EPS)
    return y * g


def _silu(x):
    h = 0.5 * x
    return h * jnp.tanh(h) + h


def _log_sigmoid(x):
    return jnp.minimum(x, 0.0) - jnp.log1p(jnp.exp(-jnp.abs(x)))


def _params(vmem, *sem):
    return pltpu.CompilerParams(dimension_semantics=sem, vmem_limit_bytes=vmem)


_RESIDENT = dict(pipeline_mode=pl.Buffered(1))


def _panel_cols(ref, lo, n):
    p, off = divmod(lo, PANEL)
    assert off + n <= PANEL
    return ref[p, :, off:off + n]


def _to_panels(w):
    K, N = w.shape
    return w.reshape(K, N // PANEL, PANEL).transpose(1, 0, 2)


def _cast_job(w, layer, grid, panels=False):
    _, R, C = w.shape
    ni = grid[0]
    rep = 1
    while (R * rep) % (ni * BF16_ROWS) and rep < ni:
        rep *= 2
    nrb = ni // rep
    assert ni % rep == 0 and R % nrb == 0
    ncol = 1
    if len(grid) == 2:
        assert not panels
        ncol = max([n for n in range(2, grid[1] + 1) if C % (n * LANES) == 0] + [1])
        idx = lambda i, j: (i // rep, jnp.minimum(j, ncol - 1))
    else:
        idx = lambda i: (i // rep, 0)
    src = pl.BlockSpec((None, R // nrb, C // ncol), lambda *ij: (layer,) + idx(*ij))
    if panels:
        assert C % PANEL == 0
        dst = pl.BlockSpec((C // PANEL, R // nrb, PANEL), lambda i: (0, i // rep, 0))
        return src, dst, jax.ShapeDtypeStruct((C // PANEL, R, PANEL), BF16), "panels"
    dst = pl.BlockSpec((R // nrb, C // ncol), idx)
    return src, dst, jax.ShapeDtypeStruct((R, C), BF16), "plain"


def _cast_job_t(wt, layer, grid, ncols):
    R = wt.shape[-1]
    assert ncols % PANEL == 0
    per = pl.cdiv(ncols // LANES, grid[0])
    lpp = PANEL // LANES
    if per <= lpp:
        per = min(d for d in range(per, lpp + 1) if lpp % d == 0)
    else:
        per = lpp * pl.cdiv(per, lpp)
    cpb = per * LANES
    last = pl.cdiv(ncols, cpb) - 1
    blk = lambda *ij: jnp.minimum(ij[0], last)
    src = pl.BlockSpec((None, cpb, R), lambda *ij: (layer, blk(*ij), 0))
    if cpb <= PANEL:
        bpp = PANEL // cpb
        dst = pl.BlockSpec((None, R, cpb), lambda *ij: (blk(*ij) // bpp, 0, blk(*ij) % bpp))
    else:
        dst = pl.BlockSpec((cpb // PANEL, R, PANEL), lambda *ij: (blk(*ij), 0, 0))
    return src, dst, jax.ShapeDtypeStruct((ncols // PANEL, R, PANEL), BF16), "transposed"


def _copy_casts(src_refs, dst_refs, kinds):
    for src_ref, dst_ref, kind in zip(src_refs, dst_refs, kinds):
        v = src_ref[...]
        if kind == "transposed":
            v = v.T
        if dst_ref.ndim == 2:
            dst_ref[...] = v.astype(BF16)
        else:
            for p in range(dst_ref.shape[0]):
                dst_ref[p] = v[:, p * PANEL:(p + 1) * PANEL].astype(BF16)


def _copy_call(job, src, n_steps):
    def body(src_ref, dst_ref):
        _copy_casts([src_ref], [dst_ref], [job[3]])

    return pl.pallas_call(
        body, grid=(n_steps,), in_specs=[job[0]], out_specs=job[1], out_shape=job[2],
        compiler_params=_params(KV_VMEM, "arbitrary"), name="weight_copy")(src)


def _gate_rows(w_in_t, n_main):
    L, _, D = w_in_t.shape
    ng = 2 * MLSTM_HEADS
    assert n_main % ng == 0

    def body(src_ref, dst_ref):
        dst_ref[...] = src_ref[...]

    return pl.pallas_call(
        body, grid=(L,),
        in_specs=[pl.BlockSpec((None, ng, D), lambda l: (l, n_main // ng, 0))],
        out_specs=pl.BlockSpec((None, ng, D), lambda l: (l, 0, 0)),
        out_shape=jax.ShapeDtypeStruct((L, ng, D), F32),
        compiler_params=_params(KV_VMEM, "arbitrary"), name="gate_rows")(w_in_t)


def _ffn_kernel(*refs, final, kinds):
    nc = len(kinds)
    h_ref, g_ref, wg_ref, wu_ref, wd_ref, fg_ref = refs[:6]
    src_refs = refs[6:6 + nc]
    o_ref = refs[6 + nc]
    dst_refs = refs[7 + nc:7 + 2 * nc]
    xn_ref = refs[7 + 2 * nc]
    j = pl.program_id(1)

    def step(first):
        if first:
            x = h_ref[...]
            xn_ref[...] = _rms(x, g_ref[...]).astype(BF16)
        xn = xn_ref[...]
        a = jnp.dot(xn, wg_ref[...], preferred_element_type=F32)
        b = jnp.dot(xn, wu_ref[...], preferred_element_type=F32)
        mid = (_silu(a) * b).astype(BF16)
        base = h_ref[...] if first else o_ref[...]
        o_ref[...] = base + 0.5 * jnp.dot(mid, wd_ref[...], preferred_element_type=F32)
        _copy_casts(src_refs, dst_refs, kinds)

    pl.when(j == 0)(functools.partial(step, True))
    pl.when(j > 0)(functools.partial(step, False))

    if final:
        @pl.when(j == pl.num_programs(1) - 1)
        def _():
            o_ref[...] = _rms(o_ref[...], fg_ref[...])


def _ffn_grid(T, F):
    tm, tf = _tile(T, FFN_TM), _tile(F, FFN_TF)
    return T // tm, F // tf


def _ffn(h, g, wg, wu, wd, fg, layer, jobs, srcs, *, final):
    T, D = h.shape
    F = wg.shape[-1]
    ni, nj = _ffn_grid(T, F)
    tm, tf = T // ni, F // nj
    outs = pl.pallas_call(
        functools.partial(_ffn_kernel, final=final, kinds=tuple(jb[3] for jb in jobs)),
        grid=(ni, nj),
        in_specs=[
            pl.BlockSpec((tm, D), lambda i, j: (i, 0)),
            pl.BlockSpec((None, 1, D), lambda i, j: (layer, 0, 0)),
            pl.BlockSpec((D, tf), lambda i, j: (0, j)),
            pl.BlockSpec((D, tf), lambda i, j: (0, j)),
            pl.BlockSpec((tf, D), lambda i, j: (j, 0)),
            pl.BlockSpec((1, D), lambda i, j: (0, 0)),
        ] + [jb[0] for jb in jobs],
        out_specs=[pl.BlockSpec((tm, D), lambda i, j: (i, 0))] + [jb[1] for jb in jobs],
        out_shape=[jax.ShapeDtypeStruct((T, D), F32)] + [jb[2] for jb in jobs],
        scratch_shapes=[pltpu.VMEM((tm, D), BF16)],
        compiler_params=_params(FFN_VMEM, "arbitrary", "arbitrary"),
        name="ffn_final" if final else "ffn",
    )(h, g, wg, wu, wd, fg, *srcs)
    return outs[0], list(outs[1:])


def _scan_rows(x, op, fill, row):
    n = x.shape[0]
    sh = 1
    while sh < n:
        x = op(x, jnp.where(row >= sh, pltpu.roll(x, sh, axis=0), fill))
        sh *= 2
    return x


def _mix_kernel(h_ref, g_ref, win_ref, wgate_ref, gbias_ref, conv_ref, hn_ref, poolw_ref,
                pscale_ref, wout_ref,
                o_ref,
                pext_ref, s2_ref, s4_ref, cext_ref, c_ref, n_ref, m_ref,
                xn_ref, qs_ref, ks_ref, y_ref,
                *, ts, W):
    H = MLSTM_HEADS
    dh = W // H
    gw = W // len(POOL_WINDOWS)
    pw = min(PANEL, W)
    PH, CH = POOL_HALO, CONV_HALO
    s = pl.program_id(1)

    @pl.when(s == 0)
    def _():
        pext_ref[0:PH, :] = jnp.zeros((PH, W), F32)
        cext_ref[0:CH, :] = jnp.zeros((CH, 2 * W), F32)
        c_ref[...] = jnp.zeros_like(c_ref)
        n_ref[...] = jnp.zeros_like(n_ref)
        m_ref[...] = jnp.zeros_like(m_ref)

    x = h_ref[...]
    xn_ref[...] = _rms(x, g_ref[...]).astype(BF16)
    row = lax.broadcasted_iota(jnp.int32, (ts, 1), 0)

    def proj(lo, n):
        return jnp.dot(xn_ref[...], _panel_cols(win_ref, lo, n), preferred_element_type=F32)

    for c0 in range(0, W, pw):
        pext_ref[PH:PH + ts, c0:c0 + pw] = proj(c0, pw)
    tpos = row + s * ts + 1
    n_ext = PH + ts
    for g, win in enumerate(POOL_WINDOWS):
        cs = slice(g * gw, (g + 1) * gw)
        pg = pext_ref[PH:n_ext, cs]
        if win == 2:
            acc = pg + pext_ref[PH - 1:n_ext - 1, cs]
        else:
            s2_ref[8:n_ext, :] = pext_ref[8:n_ext, cs] + pext_ref[7:n_ext - 1, cs]
            if win == 4:
                acc = s2_ref[PH:n_ext, :] + s2_ref[PH - 2:n_ext - 2, :]
            else:
                s4_ref[16:n_ext, :] = s2_ref[16:n_ext, :] + s2_ref[14:n_ext - 2, :]
                if win == 8:
                    acc = s4_ref[PH:n_ext, :] + s4_ref[PH - 4:n_ext - 4, :]
                else:
                    s8 = s4_ref[24:n_ext, :] + s4_ref[20:n_ext - 4, :]
                    acc = s8[8:] + s8[:-8]
        inv_cnt = 1.0 / jnp.minimum(tpos, win).astype(F32)
        d = acc * inv_cnt - pg
        yg = jnp.dot(d.astype(BF16), poolw_ref[g], preferred_element_type=F32)
        y_ref[:, cs] = (yg * pscale_ref[:, cs]).astype(BF16)
    pext_ref[0:PH, :] = pext_ref[ts:ts + PH, :]

    nblk = W // dh
    for c0 in range(0, 2 * W, pw):
        cext_ref[CH:CH + ts, c0:c0 + pw] = proj(W + c0, pw)
    for blk in range(2 * nblk):
        hs = slice(blk * dh, (blk + 1) * dh)
        lo = CH - (CONV_W - 1)
        y = cext_ref[lo:lo + ts, hs] * conv_ref[0:1, hs]
        for j in range(1, CONV_W):
            y = y + cext_ref[lo + j:lo + j + ts, hs] * conv_ref[j:j + 1, hs]
        y = _silu(y)
        cext_ref[0:CH, hs] = cext_ref[ts:ts + CH, hs]
        if blk < nblk:
            qs_ref[:, hs] = (y * (dh ** -0.5)).astype(BF16)
        else:
            ks_ref[:, (blk - nblk) * dh:(blk - nblk + 1) * dh] = y

    gates = jnp.dot(xn_ref[...], wgate_ref[...], preferred_element_type=F32) + gbias_ref[...]
    li = gates[:, :LANES]
    lf = _log_sigmoid(gates[:, LANES:])
    b = _scan_rows(lf, jnp.add, 0.0, row)
    a = li - b
    m_prev = m_ref[...]
    u = jnp.maximum(m_prev, _scan_rows(a, jnp.maximum, -jnp.inf, row))
    u_l = u[ts - 1:ts, :]
    m_ref[...] = b[ts - 1:ts, :] + u_l
    iw = jnp.exp(m_prev - u)
    emt = jnp.exp(-b - u)
    ws = jnp.exp(a - u_l)
    decay = jnp.exp(m_prev - u_l)
    a_row = a.T
    tri = (lax.broadcasted_iota(jnp.int32, (ts, ts), 0)
           >= lax.broadcasted_iota(jnp.int32, (ts, ts), 1))

    for hd in range(H):
        cs = slice(hd * dh, (hd + 1) * dh)
        q_c = qs_ref[:, cs]
        k_c = ks_ref[:, cs]
        v_c = proj(3 * W + hd * dh, dh).astype(BF16)
        og = proj(4 * W + hd * dh, dh)
        sc = lax.dot_general(q_c, k_c.astype(BF16), (((1,), (1,)), ((), ())),
                             preferred_element_type=F32)
        e = jnp.where(tri, jnp.exp(a_row[hd:hd + 1, :] - u[:, hd:hd + 1]), 0.0)
        wm = e * sc
        cst = c_ref[hd]
        nst = n_ref[hd]
        iw_h = iw[:, hd:hd + 1]
        num = (iw_h * jnp.dot(q_c, cst.astype(BF16), preferred_element_type=F32)
               + jnp.dot(wm.astype(BF16), v_c, preferred_element_type=F32))
        den = (iw_h * jnp.sum(q_c.astype(F32) * nst, axis=-1, keepdims=True)
               + jnp.sum(wm, axis=-1, keepdims=True))
        hh = num * (1.0 / jnp.maximum(jnp.abs(den), emt[:, hd:hd + 1]))
        hh = hh * lax.rsqrt(jnp.mean(hh * hh, axis=-1, keepdims=True) + EPS)
        yv = hh * hn_ref[:, cs] * jax.nn.sigmoid(og)
        y_ref[:, W + hd * dh:W + (hd + 1) * dh] = yv.astype(BF16)
        kw = k_c * ws[:, hd:hd + 1]
        dec_h = decay[:, hd:hd + 1]
        c_ref[hd] = dec_h * cst + lax.dot_general(
            kw.astype(BF16), v_c, (((0,), (0,)), ((), ())), preferred_element_type=F32)
        n_ref[hd] = dec_h * nst + jnp.sum(kw, axis=0, keepdims=True)

    for p in range(wout_ref.shape[0]):
        cs = slice(p * PANEL, (p + 1) * PANEL)
        o_ref[:, cs] = x[:, cs] + jnp.dot(y_ref[...], wout_ref[p], preferred_element_type=F32)


def _mix(h, g, w_main, w_gate, gbias, conv_w, head_norm, pool_w, pool_scale, w_out, layer, *, B, S):
    T, D = h.shape
    W = D // 2
    H = MLSTM_HEADS
    dh = W // H
    ng = len(POOL_WINDOWS)
    gw = W // ng
    ts = _tile(S, MIX_TILE)
    nst = S // ts
    return pl.pallas_call(
        functools.partial(_mix_kernel, ts=ts, W=W),
        grid=(B, nst),
        in_specs=[
            pl.BlockSpec((ts, D), lambda b, s: (b * nst + s, 0)),
            pl.BlockSpec((None, 1, D), lambda b, s: (layer, 0, 0)),
            pl.BlockSpec(w_main.shape, lambda b, s: (0, 0, 0), **_RESIDENT),
            pl.BlockSpec((None, D, 2 * LANES), lambda b, s: (layer, 0, 0), **_RESIDENT),
            pl.BlockSpec((None, 1, 2 * LANES), lambda b, s: (layer, 0, 0)),
            pl.BlockSpec((None, CONV_W, 2 * W), lambda b, s: (layer, 0, 0)),
            pl.BlockSpec((None, 1, W), lambda b, s: (layer, 0, 0)),
            pl.BlockSpec((ng, gw, gw), lambda b, s: (0, 0, 0)),
            pl.BlockSpec((None, 1, W), lambda b, s: (layer, 0, 0)),
            pl.BlockSpec(w_out.shape, lambda b, s: (0, 0, 0), **_RESIDENT),
        ],
        out_specs=pl.BlockSpec((ts, D), lambda b, s: (b * nst + s, 0)),
        out_shape=jax.ShapeDtypeStruct((T, D), F32),
        scratch_shapes=[
            pltpu.VMEM((POOL_HALO + ts, W), F32),
            pltpu.VMEM((POOL_HALO + ts, gw), F32),
            pltpu.VMEM((POOL_HALO + ts, gw), F32),
            pltpu.VMEM((CONV_HALO + ts, 2 * W), F32),
            pltpu.VMEM((H, dh, dh), F32),
            pltpu.VMEM((H, 1, dh), F32),
            pltpu.VMEM((1, LANES), F32),
            pltpu.VMEM((ts, D), BF16),
            pltpu.VMEM((ts, W), BF16),
            pltpu.VMEM((ts, W), F32),
            pltpu.VMEM((ts, D), BF16),
        ],
        compiler_params=_params(MIX_VMEM, "arbitrary", "arbitrary"),
        name="mix",
    )(h, g, w_main, w_gate, gbias, conv_w, head_norm, pool_w, pool_scale, w_out)


def _kv_kernel(m_ref, g_ref, w_ref, o_ref, xn_ref):
    j = pl.program_id(1)

    def step(first):
        if first:
            xn_ref[...] = _rms(m_ref[...], g_ref[...]).astype(BF16)
        o_ref[...] = jnp.dot(xn_ref[...], w_ref[...], preferred_element_type=F32).astype(BF16)

    pl.when(j == 0)(functools.partial(step, True))
    pl.when(j > 0)(functools.partial(step, False))


def _kv_proj(mem, g, wkv, layer):
    T, D = mem.shape
    N = wkv.shape[-1]
    tm = _tile(T, KV_TM)
    tn = _tile(N, KV_TN)
    return pl.pallas_call(
        _kv_kernel,
        grid=(T // tm, N // tn),
        in_specs=[
            pl.BlockSpec((tm, D), lambda i, j: (i, 0)),
            pl.BlockSpec((None, 1, D), lambda i, j: (layer, 0, 0)),
            pl.BlockSpec((D, tn), lambda i, j: (0, j)),
        ],
        out_specs=pl.BlockSpec((tm, tn), lambda i, j: (i, j)),
        out_shape=jax.ShapeDtypeStruct((T, N), BF16),
        scratch_shapes=[pltpu.VMEM((tm, D), BF16)],
        compiler_params=_params(KV_VMEM, "parallel", "arbitrary"),
        name="xattn_kv",
    )(mem, g, wkv)


def _xattn_kernel(*refs, D, kinds):
    nc = len(kinds)
    h_ref, g_ref, wq_ref, kv_ref, wo_ref = refs[:5]
    o_ref = refs[5 + nc]
    xn_ref, ao_ref = refs[6 + 2 * nc:]
    H = XATTN_HEADS
    dh = D // H
    x = h_ref[...]
    xn_ref[...] = _rms(x, g_ref[...]).astype(BF16)
    for hd in range(H):
        cs = slice(hd * dh, (hd + 1) * dh)
        q = jnp.dot(xn_ref[...], _panel_cols(wq_ref, hd * dh, dh), preferred_element_type=F32)
        s = lax.dot_general(q.astype(BF16), kv_ref[:, cs], (((1,), (1,)), ((), ())),
                            preferred_element_type=F32) * (dh ** -0.5)
        e = jnp.exp(s - jnp.max(s, axis=-1, keepdims=True))
        pr = e * (1.0 / jnp.sum(e, axis=-1, keepdims=True))
        ao_ref[:, cs] = jnp.dot(pr.astype(BF16), kv_ref[:, D + hd * dh:D + (hd + 1) * dh],
                                preferred_element_type=F32).astype(BF16)
    for p in range(wo_ref.shape[0]):
        cs = slice(p * PANEL, (p + 1) * PANEL)
        o_ref[:, cs] = x[:, cs] + jnp.dot(ao_ref[...], wo_ref[p], preferred_element_type=F32)
    _copy_casts(refs[5:5 + nc], refs[6 + nc:6 + 2 * nc], kinds)


def _xattn(h, g, wq, kv, wo, layer, jobs, srcs, *, S, M):
    T, D = h.shape
    tm = _tile(S, XATTN_TM)
    nsb = S // tm
    outs = pl.pallas_call(
        functools.partial(_xattn_kernel, D=D, kinds=tuple(jb[3] for jb in jobs)),
        grid=(T // tm,),
        in_specs=[
            pl.BlockSpec((tm, D), lambda i: (i, 0)),
            pl.BlockSpec((None, 1, D), lambda i: (layer, 0, 0)),
            pl.BlockSpec(wq.shape, lambda i: (0, 0, 0), **_RESIDENT),
            pl.BlockSpec((M, 2 * D), lambda i: (i // nsb, 0)),
            pl.BlockSpec(wo.shape, lambda i: (0, 0, 0), **_RESIDENT),
        ] + [jb[0] for jb in jobs],
        out_specs=[pl.BlockSpec((tm, D), lambda i: (i, 0))] + [jb[1] for jb in jobs],
        out_shape=[jax.ShapeDtypeStruct((T, D), F32)] + [jb[2] for jb in jobs],
        scratch_shapes=[pltpu.VMEM((tm, D), BF16), pltpu.VMEM((tm, D), BF16)],
        compiler_params=_params(XATTN_VMEM, "arbitrary"),
        name="xattn",
    )(h, g, wq, kv, wo, *srcs)
    return outs[0], list(outs[1:])


def kernel(x, mem, ffn1_norm, ffn1_w_gate, ffn1_w_up, ffn1_w_down, mix_norm, w_in, gate_bias, qk_conv, head_norm, pool_w, pool_scale, w_out, xattn_norm, mem_norm, xattn_wq, xattn_wkv, xattn_wo, ffn2_norm, ffn2_w_gate, ffn2_w_up, ffn2_w_down, final_norm):
    B, S, D = x.shape
    M = mem.shape[1]
    depth = ffn1_norm.shape[0]
    W = D // 2
    H = MLSTM_HEADS
    n_main = 5 * W

    bf = lambda w: w.astype(BF16)
    row = lambda v: v.reshape(v.shape[0], 1, v.shape[-1])

    w_in_t = jnp.swapaxes(w_in, 1, 2)
    lane_pad = lambda v: jnp.pad(v, [(0, 0)] * (v.ndim - 1) + [(0, LANES - H)])
    gate_cols = jnp.swapaxes(_gate_rows(w_in_t, n_main), 1, 2)
    w_gate = bf(jnp.concatenate([lane_pad(gate_cols[..., :H]), lane_pad(gate_cols[..., H:])], axis=-1))
    gbias = jnp.concatenate([lane_pad(gate_bias[:, :H]), lane_pad(gate_bias[:, H:])],
                            axis=-1).reshape(depth, 1, 2 * LANES)
    fg = final_norm.reshape(1, D)
    pool_rows = pool_w.reshape(depth, -1, pool_w.shape[-1])

    fgrid = _ffn_grid(B * S, ffn1_w_gate.shape[-1])

    def ffn_jobs(ws, l):
        return [_cast_job(w, l, fgrid) for w in ws], list(ws)

    xgrid = (B * S // _tile(S, XATTN_TM),)

    def xattn_jobs(l, last):
        srcs = [ffn2_w_gate, ffn2_w_up, ffn2_w_down]
        jobs = [_cast_job(w, l, xgrid) for w in srcs]
        if not last:
            srcs += [w_in_t, pool_rows, w_out, xattn_wq, xattn_wkv, xattn_wo]
            jobs += [_cast_job_t(w_in_t, l + 1, xgrid, n_main), _cast_job(pool_rows, l + 1, xgrid),
                     _cast_job(w_out, l + 1, xgrid, panels=True),
                     _cast_job(xattn_wq, l + 1, xgrid, panels=True),
                     _cast_job(xattn_wkv, l + 1, xgrid),
                     _cast_job(xattn_wo, l + 1, xgrid, panels=True)]
        return jobs, srcs

    fw = [bf(ffn1_w_gate[0]), bf(ffn1_w_up[0]), bf(ffn1_w_down[0])]
    mw = [None, bf(pool_rows[0]), _to_panels(bf(w_out[0])), _to_panels(bf(xattn_wq[0])),
          bf(xattn_wkv[0]), _to_panels(bf(xattn_wo[0]))]
    h = x.reshape(B * S, D)
    mem2 = mem.reshape(B * M, D)
    for l in range(depth):
        last = l == depth - 1
        h, _ = _ffn(h, row(ffn1_norm), fw[0], fw[1], fw[2], fg, l, [], [], final=False)
        if l == 0:
            n_steps = n_main // PANEL
            mw[0] = _copy_call(_cast_job_t(w_in_t, 0, (n_steps,), n_main), w_in_t, n_steps)
        w_main, poolw, wout, wq, wkv, wo = mw
        h = _mix(h, row(mix_norm), w_main, w_gate, gbias, qk_conv, row(head_norm),
                 poolw.reshape(pool_w.shape[1:]), row(pool_scale), wout, l, B=B, S=S)
        kv = _kv_proj(mem2, row(mem_norm), wkv, l)
        jobs, srcs = xattn_jobs(l, last)
        h, cw = _xattn(h, row(xattn_norm), wq, kv, wo, l, jobs, srcs, S=S, M=M)
        fw, mw = cw[:3], cw[3:]
        jobs, srcs = ([], []) if last else ffn_jobs((ffn1_w_gate, ffn1_w_up, ffn1_w_down), l + 1)
        h, cw = _ffn(h, row(ffn2_norm), fw[0], fw[1], fw[2], fg, l, jobs, srcs, final=last)
        fw = cw
    return h.reshape(B, S, D)
```

```python
import functools

import jax
import jax.numpy as jnp
from jax import lax
from jax.experimental import pallas as pl
from jax.experimental.pallas import tpu as pltpu

F32 = jnp.float32
BF16 = jnp.bfloat16

EPS = 1e-6
MLSTM_HEADS = 4
POOL_WINDOWS = (2, 4, 8, 16)
CONV_W = 4
XATTN_HEADS = 4
LANES = 128
BF16_ROWS = 16
PANEL = 512
POOL_HALO = 32
CONV_HALO = 8
MIB = 1024 * 1024

FFN_TM, FFN_TF, FFN_VMEM = 1024, 512, 60 * MIB
MIX_TILE, MIX_VMEM = 256, 56 * MIB
XATTN_TM, XATTN_VMEM = 512, 58 * MIB
KV_TM, KV_TN, KV_VMEM = 512, 1024, 32 * MIB


def _tile(n, pref):
    return pref if n % pref == 0 else n


def _rms(x, g):
    y = x * lax.rsqrt(jnp.mean(x * x, axis=-1, keepdims=True) + EPS)
    return y * g


def _silu(x):
    h = 0.5 * x
    return h * jnp.tanh(h) + h


def _log_sigmoid(x):
    return jnp.minimum(x, 0.0) - jnp.log1p(jnp.exp(-jnp.abs(x)))


def _params(vmem, *sem):
    return pltpu.CompilerParams(dimension_semantics=sem, vmem_limit_bytes=vmem)


_RESIDENT = dict(pipeline_mode=pl.Buffered(1))


def _panel_cols(ref, lo, n):
    p, off = divmod(lo, PANEL)
    assert off + n <= PANEL
    return ref[p, :, off:off + n]


def _to_panels(w):
    K, N = w.shape
    return w.reshape(K, N // PANEL, PANEL).transpose(1, 0, 2)


def _cast_job(w, layer, grid, panels=False):
    _, R, C = w.shape
    ni = grid[0]
    rep = 1
    while (R * rep) % (ni * BF16_ROWS) and rep < ni:
        rep *= 2
    nrb = ni // rep
    assert ni % rep == 0 and R % nrb == 0
    ncol = 1
    if len(grid) == 2:
        assert not panels
        ncol = max([n for n in range(2, grid[1] + 1) if C % (n * LANES) == 0] + [1])
        idx = lambda i, j: (i // rep, jnp.minimum(j, ncol - 1))
    else:
        idx = lambda i: (i // rep, 0)
    src = pl.BlockSpec((None, R // nrb, C // ncol), lambda *ij: (layer,) + idx(*ij))
    if panels:
        assert C % PANEL == 0
        dst = pl.BlockSpec((C // PANEL, R // nrb, PANEL), lambda i: (0, i // rep, 0))
        return src, dst, jax.ShapeDtypeStruct((C // PANEL, R, PANEL), BF16), "panels"
    dst = pl.BlockSpec((R // nrb, C // ncol), idx)
    return src, dst, jax.ShapeDtypeStruct((R, C), BF16), "plain"


def _cast_job_t(wt, layer, grid, ncols):
    R = wt.shape[-1]
    assert ncols % PANEL == 0
    per = pl.cdiv(ncols // LANES, grid[0])
    lpp = PANEL // LANES
    if per <= lpp:
        per = min(d for d in range(per, lpp + 1) if lpp % d == 0)
    else:
        per = lpp * pl.cdiv(per, lpp)
    cpb = per * LANES
    last = pl.cdiv(ncols, cpb) - 1
    blk = lambda *ij: jnp.minimum(ij[0], last)
    src = pl.BlockSpec((None, cpb, R), lambda *ij: (layer, blk(*ij), 0))
    if cpb <= PANEL:
        bpp = PANEL // cpb
        dst = pl.BlockSpec((None, R, cpb), lambda *ij: (blk(*ij) // bpp, 0, blk(*ij) % bpp))
    else:
        dst = pl.BlockSpec((cpb // PANEL, R, PANEL), lambda *ij: (blk(*ij), 0, 0))
    return src, dst, jax.ShapeDtypeStruct((ncols // PANEL, R, PANEL), BF16), "transposed"


def _copy_casts(src_refs, dst_refs, kinds):
    for src_ref, dst_ref, kind in zip(src_refs, dst_refs, kinds):
        v = src_ref[...]
        if kind == "transposed":
            v = v.T
        if dst_ref.ndim == 2:
            dst_ref[...] = v.astype(BF16)
        else:
            for p in range(dst_ref.shape[0]):
                dst_ref[p] = v[:, p * PANEL:(p + 1) * PANEL].astype(BF16)


def _copy_call(job, src, n_steps):
    def body(src_ref, dst_ref):
        _copy_casts([src_ref], [dst_ref], [job[3]])

    return pl.pallas_call(
        body, grid=(n_steps,), in_specs=[job[0]], out_specs=job[1], out_shape=job[2],
        compiler_params=_params(KV_VMEM, "arbitrary"), name="weight_copy")(src)


def _gate_rows(w_in_t, n_main):
    L, _, D = w_in_t.shape
    ng = 2 * MLSTM_HEADS
    assert n_main % ng == 0

    def body(src_ref, dst_ref):
        dst_ref[...] = src_ref[...]

    return pl.pallas_call(
        body, grid=(L,),
        in_specs=[pl.BlockSpec((None, ng, D), lambda l: (l, n_main // ng, 0))],
        out_specs=pl.BlockSpec((None, ng, D), lambda l: (l, 0, 0)),
        out_shape=jax.ShapeDtypeStruct((L, ng, D), F32),
        compiler_params=_params(KV_VMEM, "arbitrary"), name="gate_rows")(w_in_t)


def _ffn_kernel(*refs, final, kinds):
    nc = len(kinds)
    h_ref, g_ref, wg_ref, wu_ref, wd_ref, fg_ref = refs[:6]
    src_refs = refs[6:6 + nc]
    o_ref = refs[6 + nc]
    dst_refs = refs[7 + nc:7 + 2 * nc]
    xn_ref = refs[7 + 2 * nc]
    j = pl.program_id(1)

    def step(first):
        if first:
            x = h_ref[...]
            xn_ref[...] = _rms(x, g_ref[...]).astype(BF16)
        xn = xn_ref[...]
        a = jnp.dot(xn, wg_ref[...], preferred_element_type=F32)
        b = jnp.dot(xn, wu_ref[...], preferred_element_type=F32)
        mid = (_silu(a) * b).astype(BF16)
        base = h_ref[...] if first else o_ref[...]
        o_ref[...] = base + 0.5 * jnp.dot(mid, wd_ref[...], preferred_element_type=F32)
        _copy_casts(src_refs, dst_refs, kinds)

    pl.when(j == 0)(functools.partial(step, True))
    pl.when(j > 0)(functools.partial(step, False))

    if final:
        @pl.when(j == pl.num_programs(1) - 1)
        def _():
            o_ref[...] = _rms(o_ref[...], fg_ref[...])


def _ffn_grid(T, F):
    tm, tf = _tile(T, FFN_TM), _tile(F, FFN_TF)
    return T // tm, F // tf


def _ffn(h, g, wg, wu, wd, fg, layer, jobs, srcs, *, final):
    T, D = h.shape
    F = wg.shape[-1]
    ni, nj = _ffn_grid(T, F)
    tm, tf = T // ni, F // nj
    outs = pl.pallas_call(
        functools.partial(_ffn_kernel, final=final, kinds=tuple(jb[3] for jb in jobs)),
        grid=(ni, nj),
        in_specs=[
            pl.BlockSpec((tm, D), lambda i, j: (i, 0)),
            pl.BlockSpec((None, 1, D), lambda i, j: (layer, 0, 0)),
            pl.BlockSpec((D, tf), lambda i, j: (0, j)),
            pl.BlockSpec((D, tf), lambda i, j: (0, j)),
            pl.BlockSpec((tf, D), lambda i, j: (j, 0)),
            pl.BlockSpec((1, D), lambda i, j: (0, 0)),
        ] + [jb[0] for jb in jobs],
        out_specs=[pl.BlockSpec((tm, D), lambda i, j: (i, 0))] + [jb[1] for jb in jobs],
        out_shape=[jax.ShapeDtypeStruct((T, D), F32)] + [jb[2] for jb in jobs],
        scratch_shapes=[pltpu.VMEM((tm, D), BF16)],
        compiler_params=_params(FFN_VMEM, "arbitrary", "arbitrary"),
        name="ffn_final" if final else "ffn",
    )(h, g, wg, wu, wd, fg, *srcs)
    return outs[0], list(outs[1:])


def _scan_rows(x, op, fill, row):
    n = x.shape[0]
    sh = 1
    while sh < n:
        x = op(x, jnp.where(row >= sh, pltpu.roll(x, sh, axis=0), fill))
        sh *= 2
    return x


def _mix_kernel(h_ref, g_ref, win_ref, wgate_ref, gbias_ref, conv_ref, hn_ref, poolw_ref,
                pscale_ref, wout_ref,
                o_ref,
                pext_ref, s2_ref, s4_ref, cext_ref, c_ref, n_ref, m_ref,
                xn_ref, qs_ref, ks_ref, y_ref,
                *, ts, W):
    H = MLSTM_HEADS
    dh = W // H
    gw = W // len(POOL_WINDOWS)
    pw = min(PANEL, W)
    PH, CH = POOL_HALO, CONV_HALO
    s = pl.program_id(1)

    @pl.when(s == 0)
    def _():
        pext_ref[0:PH, :] = jnp.zeros((PH, W), F32)
        cext_ref[0:CH, :] = jnp.zeros((CH, 2 * W), F32)
        c_ref[...] = jnp.zeros_like(c_ref)
        n_ref[...] = jnp.zeros_like(n_ref)
        m_ref[...] = jnp.zeros_like(m_ref)

    x = h_ref[...]
    xn_ref[...] = _rms(x, g_ref[...]).astype(BF16)
    row = lax.broadcasted_iota(jnp.int32, (ts, 1), 0)

    def proj(lo, n):
        return jnp.dot(xn_ref[...], _panel_cols(win_ref, lo, n), preferred_element_type=F32)

    for c0 in range(0, W, pw):
        pext_ref[PH:PH + ts, c0:c0 + pw] = proj(c0, pw)
    tpos = row + s * ts + 1
    n_ext = PH + ts
    for g, win in enumerate(POOL_WINDOWS):
        cs = slice(g * gw, (g + 1) * gw)
        pg = pext_ref[PH:n_ext, cs]
        if win == 2:
            acc = pg + pext_ref[PH - 1:n_ext - 1, cs]
        else:
            s2_ref[8:n_ext, :] = pext_ref[8:n_ext, cs] + pext_ref[7:n_ext - 1, cs]
            if win == 4:
                acc = s2_ref[PH:n_ext, :] + s2_ref[PH - 2:n_ext - 2, :]
            else:
                s4_ref[16:n_ext, :] = s2_ref[16:n_ext, :] + s2_ref[14:n_ext - 2, :]
                if win == 8:
                    acc = s4_ref[PH:n_ext, :] + s4_ref[PH - 4:n_ext - 4, :]
                else:
                    s8 = s4_ref[24:n_ext, :] + s4_ref[20:n_ext - 4, :]
                    acc = s8[8:] + s8[:-8]
        inv_cnt = 1.0 / jnp.minimum(tpos, win).astype(F32)
        d = acc * inv_cnt - pg
        yg = jnp.dot(d.astype(BF16), poolw_ref[g], preferred_element_type=F32)
        y_ref[:, cs] = (yg * pscale_ref[:, cs]).astype(BF16)
    pext_ref[0:PH, :] = pext_ref[ts:ts + PH, :]

    nblk = W // dh
    for c0 in range(0, 2 * W, pw):
        cext_ref[CH:CH + ts, c0:c0 + pw] = proj(W + c0, pw)
    for blk in range(2 * nblk):
        hs = slice(blk * dh, (blk + 1) * dh)
        lo = CH - (CONV_W - 1)
        y = cext_ref[lo:lo + ts, hs] * conv_ref[0:1, hs]
        for j in range(1, CONV_W):
            y = y + cext_ref[lo + j:lo + j + ts, hs] * conv_ref[j:j + 1, hs]
        y = _silu(y)
        cext_ref[0:CH, hs] = cext_ref[ts:ts + CH, hs]
        if blk < nblk:
            qs_ref[:, hs] = (y * (dh ** -0.5)).astype(BF16)
        else:
            ks_ref[:, (blk - nblk) * dh:(blk - nblk + 1) * dh] = y

    gates = jnp.dot(xn_ref[...], wgate_ref[...], preferred_element_type=F32) + gbias_ref[...]
    li = gates[:, :LANES]
    lf = _log_sigmoid(gates[:, LANES:])
    b = _scan_rows(lf, jnp.add, 0.0, row)
    a = li - b
    m_prev = m_ref[...]
    u = jnp.maximum(m_prev, _scan_rows(a, jnp.maximum, -jnp.inf, row))
    u_l = u[ts - 1:ts, :]
    m_ref[...] = b[ts - 1:ts, :] + u_l
    iw = jnp.exp(m_prev - u)
    emt = jnp.exp(-b - u)
    ws = jnp.exp(a - u_l)
    decay = jnp.exp(m_prev - u_l)
    a_row = a.T
    tri = (lax.broadcasted_iota(jnp.int32, (ts, ts), 0)
           >= lax.broadcasted_iota(jnp.int32, (ts, ts), 1))

    for hd in range(H):
        cs = slice(hd * dh, (hd + 1) * dh)
        q_c = qs_ref[:, cs]
        k_c = ks_ref[:, cs]
        v_c = proj(3 * W + hd * dh, dh).astype(BF16)
        og = proj(4 * W + hd * dh, dh)
        sc = lax.dot_general(q_c, k_c.astype(BF16), (((1,), (1,)), ((), ())),
                             preferred_element_type=F32)
        e = jnp.where(tri, jnp.exp(a_row[hd:hd + 1, :] - u[:, hd:hd + 1]), 0.0)
        wm = e * sc
        cst = c_ref[hd]
        nst = n_ref[hd]
        iw_h = iw[:, hd:hd + 1]
        num = (iw_h * jnp.dot(q_c, cst.astype(BF16), preferred_element_type=F32)
               + jnp.dot(wm.astype(BF16), v_c, preferred_element_type=F32))
        den = (iw_h * jnp.sum(q_c.astype(F32) * nst, axis=-1, keepdims=True)
               + jnp.sum(wm, axis=-1, keepdims=True))
        hh = num * (1.0 / jnp.maximum(jnp.abs(den), emt[:, hd:hd + 1]))
        hh = hh * lax.rsqrt(jnp.mean(hh * hh, axis=-1, keepdims=True) + EPS)
        yv = hh * hn_ref[:, cs] * (0.5 * jnp.tanh(0.5 * og) + 0.5)
        y_ref[:, W + hd * dh:W + (hd + 1) * dh] = yv.astype(BF16)
        kw = k_c * ws[:, hd:hd + 1]
        dec_h = decay[:, hd:hd + 1]
        c_ref[hd] = dec_h * cst + lax.dot_general(
            kw.astype(BF16), v_c, (((0,), (0,)), ((), ())), preferred_element_type=F32)
        n_ref[hd] = dec_h * nst + jnp.sum(kw, axis=0, keepdims=True)

    for p in range(wout_ref.shape[0]):
        cs = slice(p * PANEL, (p + 1) * PANEL)
        o_ref[:, cs] = x[:, cs] + jnp.dot(y_ref[...], wout_ref[p], preferred_element_type=F32)


def _mix(h, g, w_main, w_gate, gbias, conv_w, head_norm, pool_w, pool_scale, w_out, layer, *, B, S):
    T, D = h.shape
    W = D // 2
    H = MLSTM_HEADS
    dh = W // H
    ng = len(POOL_WINDOWS)
    gw = W // ng
    ts = _tile(S, MIX_TILE)
    nst = S // ts
    return pl.pallas_call(
        functools.partial(_mix_kernel, ts=ts, W=W),
        grid=(B, nst),
        in_specs=[
            pl.BlockSpec((ts, D), lambda b, s: (b * nst + s, 0)),
            pl.BlockSpec((None, 1, D), lambda b, s: (layer, 0, 0)),
            pl.BlockSpec(w_main.shape, lambda b, s: (0, 0, 0), **_RESIDENT),
            pl.BlockSpec((None, D, 2 * LANES), lambda b, s: (layer, 0, 0), **_RESIDENT),
            pl.BlockSpec((None, 1, 2 * LANES), lambda b, s: (layer, 0, 0)),
            pl.BlockSpec((None, CONV_W, 2 * W), lambda b, s: (layer, 0, 0)),
            pl.BlockSpec((None, 1, W), lambda b, s: (layer, 0, 0)),
            pl.BlockSpec((ng, gw, gw), lambda b, s: (0, 0, 0)),
            pl.BlockSpec((None, 1, W), lambda b, s: (layer, 0, 0)),
            pl.BlockSpec(w_out.shape, lambda b, s: (0, 0, 0), **_RESIDENT),
        ],
        out_specs=pl.BlockSpec((ts, D), lambda b, s: (b * nst + s, 0)),
        out_shape=jax.ShapeDtypeStruct((T, D), F32),
        scratch_shapes=[
            pltpu.VMEM((POOL_HALO + ts, W), F32),
            pltpu.VMEM((POOL_HALO + ts, gw), F32),
            pltpu.VMEM((POOL_HALO + ts, gw), F32),
            pltpu.VMEM((CONV_HALO + ts, 2 * W), F32),
            pltpu.VMEM((H, dh, dh), F32),
            pltpu.VMEM((H, 1, dh), F32),
            pltpu.VMEM((1, LANES), F32),
            pltpu.VMEM((ts, D), BF16),
            pltpu.VMEM((ts, W), BF16),
            pltpu.VMEM((ts, W), F32),
            pltpu.VMEM((ts, D), BF16),
        ],
        compiler_params=_params(MIX_VMEM, "arbitrary", "arbitrary"),
        name="mix",
    )(h, g, w_main, w_gate, gbias, conv_w, head_norm, pool_w, pool_scale, w_out)


def _kv_kernel(m_ref, g_ref, w_ref, o_ref, xn_ref):
    j = pl.program_id(1)

    def step(first):
        if first:
            xn_ref[...] = _rms(m_ref[...], g_ref[...]).astype(BF16)
        o_ref[...] = jnp.dot(xn_ref[...], w_ref[...], preferred_element_type=F32).astype(BF16)

    pl.when(j == 0)(functools.partial(step, True))
    pl.when(j > 0)(functools.partial(step, False))


def _kv_proj(mem, g, wkv, layer):
    T, D = mem.shape
    N = wkv.shape[-1]
    tm = _tile(T, KV_TM)
    tn = _tile(N, KV_TN)
    return pl.pallas_call(
        _kv_kernel,
        grid=(T // tm, N // tn),
        in_specs=[
            pl.BlockSpec((tm, D), lambda i, j: (i, 0)),
            pl.BlockSpec((None, 1, D), lambda i, j: (layer, 0, 0)),
            pl.BlockSpec((D, tn), lambda i, j: (0, j)),
        ],
        out_specs=pl.BlockSpec((tm, tn), lambda i, j: (i, j)),
        out_shape=jax.ShapeDtypeStruct((T, N), BF16),
        scratch_shapes=[pltpu.VMEM((tm, D), BF16)],
        compiler_params=_params(KV_VMEM, "parallel", "arbitrary"),
        name="xattn_kv",
    )(mem, g, wkv)


def _xattn_kernel(*refs, D, kinds):
    nc = len(kinds)
    h_ref, g_ref, wq_ref, kv_ref, wo_ref = refs[:5]
    o_ref = refs[5 + nc]
    xn_ref, ao_ref = refs[6 + 2 * nc:]
    H = XATTN_HEADS
    dh = D // H
    x = h_ref[...]
    xn_ref[...] = _rms(x, g_ref[...]).astype(BF16)
    for hd in range(H):
        cs = slice(hd * dh, (hd + 1) * dh)
        q = jnp.dot(xn_ref[...], _panel_cols(wq_ref, hd * dh, dh), preferred_element_type=F32)
        s = lax.dot_general(q.astype(BF16), kv_ref[:, cs], (((1,), (1,)), ((), ())),
                            preferred_element_type=F32) * (dh ** -0.5)
        e = jnp.exp(s - jnp.max(s, axis=-1, keepdims=True))
        pr = e * (1.0 / jnp.sum(e, axis=-1, keepdims=True))
        ao_ref[:, cs] = jnp.dot(pr.astype(BF16), kv_ref[:, D + hd * dh:D + (hd + 1) * dh],
                                preferred_element_type=F32).astype(BF16)
    for p in range(wo_ref.shape[0]):
        cs = slice(p * PANEL, (p + 1) * PANEL)
        o_ref[:, cs] = x[:, cs] + jnp.dot(ao_ref[...], wo_ref[p], preferred_element_type=F32)
    _copy_casts(refs[5:5 + nc], refs[6 + nc:6 + 2 * nc], kinds)


def _xattn(h, g, wq, kv, wo, layer, jobs, srcs, *, S, M):
    T, D = h.shape
    tm = _tile(S, XATTN_TM)
    nsb = S // tm
    outs = pl.pallas_call(
        functools.partial(_xattn_kernel, D=D, kinds=tuple(jb[3] for jb in jobs)),
        grid=(T // tm,),
        in_specs=[
            pl.BlockSpec((tm, D), lambda i: (i, 0)),
            pl.BlockSpec((None, 1, D), lambda i: (layer, 0, 0)),
            pl.BlockSpec(wq.shape, lambda i: (0, 0, 0), **_RESIDENT),
            pl.BlockSpec((M, 2 * D), lambda i: (i // nsb, 0)),
            pl.BlockSpec(wo.shape, lambda i: (0, 0, 0), **_RESIDENT),
        ] + [jb[0] for jb in jobs],
        out_specs=[pl.BlockSpec((tm, D), lambda i: (i, 0))] + [jb[1] for jb in jobs],
        out_shape=[jax.ShapeDtypeStruct((T, D), F32)] + [jb[2] for jb in jobs],
        scratch_shapes=[pltpu.VMEM((tm, D), BF16), pltpu.VMEM((tm, D), BF16)],
        compiler_params=_params(XATTN_VMEM, "arbitrary"),
        name="xattn",
    )(h, g, wq, kv, wo, *srcs)
    return outs[0], list(outs[1:])


def kernel(x, mem, ffn1_norm, ffn1_w_gate, ffn1_w_up, ffn1_w_down, mix_norm, w_in, gate_bias, qk_conv, head_norm, pool_w, pool_scale, w_out, xattn_norm, mem_norm, xattn_wq, xattn_wkv, xattn_wo, ffn2_norm, ffn2_w_gate, ffn2_w_up, ffn2_w_down, final_norm):
    B, S, D = x.shape
    M = mem.shape[1]
    depth = ffn1_norm.shape[0]
    W = D // 2
    H = MLSTM_HEADS
    n_main = 5 * W

    bf = lambda w: w.astype(BF16)
    row = lambda v: v.reshape(v.shape[0], 1, v.shape[-1])

    w_in_t = jnp.swapaxes(w_in, 1, 2)
    lane_pad = lambda v: jnp.pad(v, [(0, 0)] * (v.ndim - 1) + [(0, LANES - H)])
    gate_cols = jnp.swapaxes(_gate_rows(w_in_t, n_main), 1, 2)
    w_gate = bf(jnp.concatenate([lane_pad(gate_cols[..., :H]), lane_pad(gate_cols[..., H:])], axis=-1))
    gbias = jnp.concatenate([lane_pad(gate_bias[:, :H]), lane_pad(gate_bias[:, H:])],
                            axis=-1).reshape(depth, 1, 2 * LANES)
    fg = final_norm.reshape(1, D)
    pool_rows = pool_w.reshape(depth, -1, pool_w.shape[-1])

    fgrid = _ffn_grid(B * S, ffn1_w_gate.shape[-1])

    def ffn_jobs(ws, l):
        return [_cast_job(w, l, fgrid) for w in ws], list(ws)

    xgrid = (B * S // _tile(S, XATTN_TM),)

    def xattn_jobs(l, last):
        srcs = [ffn2_w_gate, ffn2_w_up, ffn2_w_down]
        jobs = [_cast_job(w, l, xgrid) for w in srcs]
        if not last:
            srcs += [w_in_t, pool_rows, w_out, xattn_wq, xattn_wkv, xattn_wo]
            jobs += [_cast_job_t(w_in_t, l + 1, xgrid, n_main), _cast_job(pool_rows, l + 1, xgrid),
                     _cast_job(w_out, l + 1, xgrid, panels=True),
                     _cast_job(xattn_wq, l + 1, xgrid, panels=True),
                     _cast_job(xattn_wkv, l + 1, xgrid),
                     _cast_job(xattn_wo, l + 1, xgrid, panels=True)]
        return jobs, srcs

    fw = [bf(ffn1_w_gate[0]), bf(ffn1_w_up[0]), bf(ffn1_w_down[0])]
    mw = [None, bf(pool_rows[0]), _to_panels(bf(w_out[0])), _to_panels(bf(xattn_wq[0])),
          bf(xattn_wkv[0]), _to_panels(bf(xattn_wo[0]))]
    h = x.reshape(B * S, D)
    mem2 = mem.reshape(B * M, D)
    for l in range(depth):
        last = l == depth - 1
        h, _ = _ffn(h, row(ffn1_norm), fw[0], fw[1], fw[2], fg, l, [], [], final=False)
        if l == 0:
            n_steps = n_main // PANEL
            mw[0] = _copy_call(_cast_job_t(w_in_t, 0, (n_steps,), n_main), w_in_t, n_steps)
        w_main, poolw, wout, wq, wkv, wo = mw
        h = _mix(h, row(mix_norm), w_main, w_gate, gbias, qk_conv, row(head_norm),
                 poolw.reshape(pool_w.shape[1:]), row(pool_scale), wout, l, B=B, S=S)
        kv = _kv_proj(mem2, row(mem_norm), wkv, l)
        jobs, srcs = xattn_jobs(l, last)
        h, cw = _xattn(h, row(xattn_norm), wq, kv, wo, l, jobs, srcs, S=S, M=M)
        fw, mw = cw[:3], cw[3:]
        jobs, srcs = ([], []) if last else ffn_jobs((ffn1_w_gate, ffn1_w_up, ffn1_w_down), l + 1)
        h, cw = _ffn(h, row(ffn2_norm), fw[0], fw[1], fw[2], fg, l, jobs, srcs, final=last)
        fw = cw
    return h.reshape(B, S, D)
```
